```python
import math
import jax, jax.numpy as jnp
from jax import lax
import numpy as np

D_MODEL = 4096
BATCH = 4
SEQ = 2048
DEPTH = 4
DEC_BATCH = 128
DEC_SEQ = 4
PAST_LEN = 16384
PAGE_SIZE = 128

D_RNN = D_MODEL // 2
RNN_BLOCKS = 16
RNN_BLOCK = D_RNN // RNN_BLOCKS
CONV_W = 4
LRU_C = 8.0
GLA_HEADS = 8
GLA_DK = D_MODEL // 4 // GLA_HEADS
GLA_DV = D_MODEL // 2 // GLA_HEADS
GLA_RANK = 16
GLA_TAU = 16.0
GLA_CHUNK = 16
D_FF = ((8 * D_MODEL // 3 + 255) // 256) * 256
N_EXPERTS = 8
TOP_K = 2
D_PLE = 256
N_DENSE = (DEPTH + 1) // 2
N_MOE = DEPTH // 2
DN_ALPHA = (2 * DEPTH) ** 0.25
DN_BETA = (8 * DEPTH) ** -0.25
LN_EPS = 1e-5
SPLIT_POINTS = (D_RNN, 2 * D_RNN, 2 * D_RNN + GLA_HEADS * GLA_DK, 2 * D_RNN + 2 * GLA_HEADS * GLA_DK,
                2 * D_RNN + 2 * GLA_HEADS * GLA_DK + GLA_HEADS * GLA_DV,
                2 * D_RNN + 2 * GLA_HEADS * GLA_DK + 2 * GLA_HEADS * GLA_DV)
D_IN = SPLIT_POINTS[-1] + GLA_RANK

kernel_name = 'hybrid_rglru_gla_deepnorm_step'


def layer_norm(x, g, b):
    xf = x.astype(jnp.float32)
    mu = jnp.mean(xf, axis=-1, keepdims=True)
    var = jnp.mean(jnp.square(xf - mu), axis=-1, keepdims=True)
    y = (xf - mu) * lax.rsqrt(var + LN_EPS) * g.astype(jnp.float32) + b.astype(jnp.float32)
    return y.astype(x.dtype)


def causal_conv(u, buf, w, b):
    L = u.shape[1]
    up = jnp.concatenate([buf.astype(u.dtype), u], axis=1)
    out = b
    for j in range(CONV_W):
        out = out + w[j] * up[:, j:j + L]
    return out, up[:, L:]


def rg_lru(u, h0, w_a, b_a, w_i, b_i, lam):
    B, L, _ = u.shape
    ub = u.reshape(B, L, RNN_BLOCKS, RNN_BLOCK)
    gate_a = jnp.einsum('blni,nij->blnj', ub, w_a).reshape(B, L, D_RNN) + b_a
    gate_i = jnp.einsum('blni,nij->blnj', ub, w_i).reshape(B, L, D_RNN) + b_i
    r = jax.nn.sigmoid(gate_a.astype(jnp.float32))
    i = jax.nn.sigmoid(gate_i.astype(jnp.float32))
    log_a = -LRU_C * r * jax.nn.softplus(-lam.astype(jnp.float32))
    a = jnp.exp(log_a)
    b_in = jnp.sqrt(-jnp.expm1(2.0 * log_a)) * i * u.astype(jnp.float32)

    def combine(lhs, rhs):
        a1, b1 = lhs
        a2, b2 = rhs
        return a1 * a2, a2 * b1 + b2

    a_cum, b_cum = lax.associative_scan(combine, (a, b_in), axis=1)
    h = a_cum * h0.astype(jnp.float32)[:, None, :] + b_cum
    return h.astype(u.dtype), h[:, -1].astype(h0.dtype)


def gla(q, k, v, log_g, s0):
    B, L, H, DK = q.shape
    DV = v.shape[-1]
    C = math.gcd(L, GLA_CHUNK)
    N = L // C

    def chunks(t):
        return t.astype(jnp.float32).reshape(B, N, C, H, t.shape[-1]).transpose(1, 0, 3, 2, 4)

    qc = chunks(q) * (DK ** -0.5)
    kc, vc, gc = chunks(k), chunks(v), chunks(log_g)
    bcum = jnp.cumsum(gc, axis=3)
    b_last = bcum[:, :, :, -1:, :]
    q_dec = qc * jnp.exp(bcum)
    k_dec = kc * jnp.exp(-bcum)
    k_end = kc * jnp.exp(b_last - bcum)
    causal = jnp.tril(jnp.ones((C, C), dtype=bool))
    scores = jnp.where(causal, jnp.einsum('nbhtd,nbhsd->nbhts', q_dec, k_dec), 0.0)
    o_intra = jnp.einsum('nbhts,nbhsv->nbhtv', scores, vc)

    def step(S, xs):
        q_i, k_i, v_i, dec_i = xs
        o_i = jnp.einsum('bhtd,bhdv->bhtv', q_i, S)
        S = S * dec_i[..., None] + jnp.einsum('bhsd,bhsv->bhdv', k_i, v_i)
        return S, o_i

    s_fin, o_inter = lax.scan(step, s0.astype(jnp.float32),
                              (q_dec, k_end, vc, jnp.exp(b_last[:, :, :, 0, :])))
    o = (o_intra + o_inter).transpose(1, 0, 3, 2, 4).reshape(B, L, H, DV)
    return o, s_fin.astype(s0.dtype)


def token_mixer(x, conv0, h0, s0, w_in, conv_w, conv_b, w_a, b_a, w_i, b_i, lam,
                w_alpha, b_alpha, gla_norm, w_proj_a, w_proj_b, w_gate, b_gate, w_out):
    B, L, _ = x.shape
    z = x @ w_in
    x_rnn, y_rnn, q, k, v, g_out, a_low = jnp.split(z, SPLIT_POINTS, axis=-1)
    xc, conv_new = causal_conv(x_rnn, conv0, conv_w, conv_b)
    h, h_new = rg_lru(xc, h0, w_a, b_a, w_i, b_i, lam)
    out_a = jax.nn.gelu(y_rnn) * h
    log_g = jax.nn.log_sigmoid((a_low @ w_alpha + b_alpha).astype(jnp.float32)) / GLA_TAU
    o, s_new = gla(q.reshape(B, L, GLA_HEADS, GLA_DK), k.reshape(B, L, GLA_HEADS, GLA_DK),
                   v.reshape(B, L, GLA_HEADS, GLA_DV), log_g.reshape(B, L, GLA_HEADS, GLA_DK), s0)
    o = o * lax.rsqrt(jnp.mean(jnp.square(o), axis=-1, keepdims=True) + LN_EPS)
    out_b = (o.reshape(B, L, GLA_HEADS * GLA_DV) * gla_norm).astype(x.dtype) * jax.nn.silu(g_out)
    gates = jax.nn.sigmoid(x @ w_gate + b_gate)
    merged = gates[..., :D_MODEL] * (out_a @ w_proj_a) + gates[..., D_MODEL:] * (out_b @ w_proj_b)
    return merged @ w_out, conv_new, h_new, s_new


def swiglu(x, w1, w3, w2):
    return (jax.nn.silu(x @ w1) * (x @ w3)) @ w2


def moe_ffn(x, w_router, w1, w3, w2):
    probs = jax.nn.softmax((x @ w_router).astype(jnp.float32), axis=-1)
    top_p, top_i = lax.top_k(probs, TOP_K)
    top_p = top_p / jnp.sum(top_p, axis=-1, keepdims=True)
    combine = jnp.sum(jax.nn.one_hot(top_i, N_EXPERTS, dtype=jnp.float32) * top_p[..., None],
                      axis=-2).astype(x.dtype)
    y = jnp.zeros_like(x)
    for e in range(N_EXPERTS):
        y = y + combine[..., e:e + 1] * swiglu(x, w1[e], w3[e], w2[e])
    return y


def trunk(x, p, conv0, h0, s0, mix_w, norm_w, dense_w, moe_w, ple_w):
    ln1_g, ln1_b, ln2_g, ln2_b = norm_w
    dense_w1, dense_w3, dense_w2 = dense_w
    moe_router, moe_w1, moe_w3, moe_w2 = moe_w
    w_ple, w_ple_gate = ple_w
    convs, hs, ss = [], [], []
    for i in range(DEPTH):
        mix, conv_new, h_new, s_new = token_mixer(x, conv0[i], h0[i], s0[i], *[w[i] for w in mix_w])
        convs.append(conv_new)
        hs.append(h_new)
        ss.append(s_new)
        x = layer_norm(DN_ALPHA * x + mix, ln1_g[i], ln1_b[i])
        j = i // 2
        if i % 2 == 0:
            ffn = swiglu(x, dense_w1[j], dense_w3[j], dense_w2[j])
        else:
            ffn = moe_ffn(x, moe_router[j], moe_w1[j], moe_w3[j], moe_w2[j])
        ple = jax.nn.sigmoid(x @ w_ple_gate[i]) * (p[i] @ w_ple[i])
        x = layer_norm(DN_ALPHA * x + ffn + ple, ln2_g[i], ln2_b[i])
    return x, jnp.stack(convs), jnp.stack(hs), jnp.stack(ss)


def setup_inputs(seed: int = 0) -> dict:
    key = jax.random.key(seed)
    ks = iter(jax.random.split(key, 48))

    def nrm(shape, scale):
        return jax.random.normal(next(ks), shape, jnp.float32) * scale

    u = jax.random.uniform(next(ks), (DEPTH, D_RNN), jnp.float32, minval=0.9, maxval=0.999)
    a0 = u ** (1.0 / LRU_C)
    lam = jnp.log(a0) - jnp.log1p(-a0)
    return {
        'x_prompt': nrm((BATCH, SEQ, D_MODEL), 1.0),
        'x_sample': nrm((DEC_BATCH, DEC_SEQ, D_MODEL), 1.0),
        'state_conv': nrm((DEPTH, DEC_BATCH, CONV_W - 1, D_RNN), 1.0),
        'state_rglru_h': nrm((DEPTH, DEC_BATCH, D_RNN), 0.5),
        'state_gla': nrm((DEPTH, DEC_BATCH, GLA_HEADS, GLA_DK, GLA_DV), 0.3),
        'p_prompt': nrm((DEPTH, BATCH, SEQ, D_PLE), 1.0),
        'p_sample': nrm((DEPTH, DEC_BATCH, DEC_SEQ, D_PLE), 1.0),
        'w_in': nrm((DEPTH, D_MODEL, D_IN), D_MODEL ** -0.5),
        'conv_w': nrm((DEPTH, CONV_W, D_RNN), CONV_W ** -0.5),
        'conv_b': nrm((DEPTH, D_RNN), 0.01),
        'rglru_w_a': nrm((DEPTH, RNN_BLOCKS, RNN_BLOCK, RNN_BLOCK), RNN_BLOCK ** -0.5),
        'rglru_b_a': nrm((DEPTH, D_RNN), 0.01),
        'rglru_w_i': nrm((DEPTH, RNN_BLOCKS, RNN_BLOCK, RNN_BLOCK), RNN_BLOCK ** -0.5),
        'rglru_b_i': nrm((DEPTH, D_RNN), 0.01),
        'rglru_lambda': lam,
        'gla_w_alpha': nrm((DEPTH, GLA_RANK, GLA_HEADS * GLA_DK), GLA_RANK ** -0.5),
        'gla_b_alpha': nrm((DEPTH, GLA_HEADS * GLA_DK), 0.5),
        'gla_norm': 1.0 + nrm((DEPTH, GLA_HEADS * GLA_DV), 0.01),
        'w_proj_a': nrm((DEPTH, D_RNN, D_MODEL), D_RNN ** -0.5 * DN_BETA),
        'w_proj_b': nrm((DEPTH, GLA_HEADS * GLA_DV, D_MODEL), (GLA_HEADS * GLA_DV) ** -0.5 * DN_BETA),
        'w_gate': nrm((DEPTH, D_MODEL, 2 * D_MODEL), D_MODEL ** -0.5),
        'b_gate': nrm((DEPTH, 2 * D_MODEL), 0.01),
        'w_out': nrm((DEPTH, D_MODEL, D_MODEL), D_MODEL ** -0.5 * DN_BETA),
        'ln1_g': 1.0 + nrm((DEPTH, D_MODEL), 0.01),
        'ln1_b': nrm((DEPTH, D_MODEL), 0.01),
        'ln2_g': 1.0 + nrm((DEPTH, D_MODEL), 0.01),
        'ln2_b': nrm((DEPTH, D_MODEL), 0.01),
        'dense_w1': nrm((N_DENSE, D_MODEL, D_FF), D_MODEL ** -0.5),
        'dense_w3': nrm((N_DENSE, D_MODEL, D_FF), D_MODEL ** -0.5),
        'dense_w2': nrm((N_DENSE, D_FF, D_MODEL), D_FF ** -0.5 * DN_BETA),
        'moe_router': nrm((N_MOE, D_MODEL, N_EXPERTS), D_MODEL ** -0.5),
        'moe_w1': nrm((N_MOE, N_EXPERTS, D_MODEL, D_FF), D_MODEL ** -0.5),
        'moe_w3': nrm((N_MOE, N_EXPERTS, D_MODEL, D_FF), D_MODEL ** -0.5),
        'moe_w2': nrm((N_MOE, N_EXPERTS, D_FF, D_MODEL), D_FF ** -0.5 * DN_BETA),
        'w_ple': nrm((DEPTH, D_PLE, D_MODEL), D_PLE ** -0.5 * DN_BETA),
        'w_ple_gate': nrm((DEPTH, D_MODEL, D_MODEL), D_MODEL ** -0.5),
    }


def reference(x_prompt, x_sample, state_conv, state_rglru_h, state_gla, p_prompt, p_sample,
              w_in, conv_w, conv_b, rglru_w_a, rglru_b_a, rglru_w_i, rglru_b_i, rglru_lambda,
              gla_w_alpha, gla_b_alpha, gla_norm, w_proj_a, w_proj_b, w_gate, b_gate, w_out,
              ln1_g, ln1_b, ln2_g, ln2_b, dense_w1, dense_w3, dense_w2,
              moe_router, moe_w1, moe_w3, moe_w2, w_ple, w_ple_gate):
    mix_w = (w_in, conv_w, conv_b, rglru_w_a, rglru_b_a, rglru_w_i, rglru_b_i, rglru_lambda,
             gla_w_alpha, gla_b_alpha, gla_norm, w_proj_a, w_proj_b, w_gate, b_gate, w_out)
    norm_w = (ln1_g, ln1_b, ln2_g, ln2_b)
    dense_w = (dense_w1, dense_w3, dense_w2)
    moe_w = (moe_router, moe_w1, moe_w3, moe_w2)
    ple_w = (w_ple, w_ple_gate)
    nb = x_prompt.shape[0]
    conv0 = jnp.zeros((DEPTH, nb, CONV_W - 1, D_RNN), x_prompt.dtype)
    h0 = jnp.zeros((DEPTH, nb, D_RNN), state_rglru_h.dtype)
    s0 = jnp.zeros((DEPTH, nb, GLA_HEADS, GLA_DK, GLA_DV), state_gla.dtype)
    y_prompt, conv_p, h_p, s_p = trunk(x_prompt, p_prompt, conv0, h0, s0, mix_w, norm_w, dense_w, moe_w, ple_w)
    y_sample, conv_s, h_s, s_s = trunk(x_sample, p_sample, state_conv, state_rglru_h, state_gla,
                                       mix_w, norm_w, dense_w, moe_w, ple_w)
    return (y_prompt, y_sample, conv_p, h_p, s_p, conv_s, h_s, s_s)
```

```python
import functools

import jax
import jax.numpy as jnp
from jax import lax
from jax.experimental import pallas as pl
from jax.experimental.pallas import tpu as pltpu

F32 = jnp.float32
BF16 = jnp.bfloat16

D_MODEL = 4096
DEPTH = 4
D_RNN = D_MODEL // 2
RNN_BLOCKS = 16
RNN_BLOCK = D_RNN // RNN_BLOCKS
CONV_W = 4
LRU_C = 8.0
GLA_HEADS = 8
GLA_DK = D_MODEL // 4 // GLA_HEADS
GLA_DV = D_MODEL // 2 // GLA_HEADS
GLA_RANK = 16
GLA_TAU = 16.0
D_FF = ((8 * D_MODEL // 3 + 255) // 256) * 256
N_EXPERTS = 8
D_PLE = 256
DN_ALPHA = (2 * DEPTH) ** 0.25
LN_EPS = 1e-5
D_QK = GLA_HEADS * GLA_DK
D_V = GLA_HEADS * GLA_DV
Z_MAIN = 2 * D_RNN + 2 * D_QK + 2 * D_V

V7X_LANES = 128
V7X_SUBLANES = 8
V7X_VMEM_LIMIT = 56 * 1024 * 1024

TOKEN_SLAB = 2176
ROW_SUB = 544
D_FF_PAD = 11264
MOE_TILE = 512
SEQ_BLOCK = 256
RNN_COLS = 512
GLA_CHUNK = 32
PAD_L = 8


def _cparams(n_axes):
    return pltpu.CompilerParams(dimension_semantics=("arbitrary",) * n_axes,
                                vmem_limit_bytes=V7X_VMEM_LIMIT)


def _sigmoid(x):
    return 1.0 / (1.0 + jnp.exp(-x))


def _silu(x):
    return x * _sigmoid(x)


def _log_sigmoid(x):
    return jnp.minimum(x, 0.0) - jnp.log1p(jnp.exp(-jnp.abs(x)))


def _linear_body(*refs, nw, kinds, epilogue, rs, tm, tn, n_valid):
    x_ref = refs[0]
    w_refs = refs[1:1 + nw]
    ex_refs = refs[1 + nw:1 + nw + len(kinds)]
    o_ref = refs[1 + nw + len(kinds)]
    wb_refs = refs[2 + nw + len(kinds):]
    for w_ref, wb in zip(w_refs, wb_refs):
        wb[...] = w_ref[...].astype(BF16)
    j = pl.program_id(1)

    def step(r, carry):
        rows = pl.ds(pl.multiple_of(r * rs, rs), rs)
        xs = x_ref[rows, :].astype(BF16)
        accs = [jnp.dot(xs, wb[...], preferred_element_type=F32) for wb in wb_refs]
        ex = [e[...] if kind == "row" else e[rows, :] for e, kind in zip(ex_refs, kinds)]
        val = epilogue(*accs, *ex)
        if n_valid is not None:
            col = j * tn + lax.broadcasted_iota(jnp.int32, val.shape, 1)
            val = jnp.where(col < n_valid, val, 0.0)
        o_ref[rows, :] = val.astype(o_ref.dtype)
        return carry

    lax.fori_loop(0, tm // rs, step, 0)


def _linear(x, ws, *, name, n_out, tn, epilogue, out_dtype, w_col0=0, extras=(), n_valid=None,
            tm=TOKEN_SLAB, rs=ROW_SUB):
    M, K = x.shape
    assert M % tm == 0 and tm % rs == 0 and n_out % tn == 0 and w_col0 % tn == 0
    wo = w_col0 // tn
    if n_valid is None:
        w_col = lambda j: j + wo
    else:
        assert n_valid % tn == 0
        last = n_valid // tn - 1
        w_col = lambda j: jnp.minimum(j + wo, last)
    in_specs = [pl.BlockSpec((tm, K), lambda i, j: (i, 0), pipeline_mode=pl.Buffered(1))]
    in_specs += [pl.BlockSpec((None,) * len(lead) + (K, tn),
                              lambda i, j, lead=lead: (*lead, 0, w_col(j))) for _, lead in ws]
    kinds = []
    args = [x, *[w for w, _ in ws]]
    for arr, kind, col0 in extras:
        assert col0 % tn == 0
        co = col0 // tn
        if kind == "row":
            in_specs.append(pl.BlockSpec((1, tn), lambda i, j, co=co: (0, j + co)))
        else:
            in_specs.append(pl.BlockSpec((tm, tn), lambda i, j, co=co: (i, j + co)))
        kinds.append(kind)
        args.append(arr)
    body = functools.partial(_linear_body, nw=len(ws), kinds=tuple(kinds), epilogue=epilogue,
                             rs=rs, tm=tm, tn=tn, n_valid=n_valid)
    return pl.pallas_call(
        body,
        grid=(M // tm, n_out // tn),
        in_specs=in_specs,
        out_specs=pl.BlockSpec((tm, tn), lambda i, j: (i, j)),
        out_shape=jax.ShapeDtypeStruct((M, n_out), out_dtype),
        scratch_shapes=[pltpu.VMEM((K, tn), BF16) for _ in ws],
        compiler_params=_cparams(2),
        name=name,
    )(*args)


DOWN_TK = 1024
DOWN_WK = 256
DOWN_NW = DOWN_TK // DOWN_WK


def _down_body(h_ref, *refs, rs, tm):
    w_refs = refs[:DOWN_NW]
    o_ref = refs[DOWN_NW]
    wb = refs[DOWN_NW + 1]
    k = pl.program_id(2)
    for q, w_ref in enumerate(w_refs):
        wb[q * DOWN_WK:(q + 1) * DOWN_WK, :] = w_ref[...].astype(BF16)

    @pl.when(k == 0)
    def _():
        o_ref[...] = jnp.zeros_like(o_ref)

    def step(r, carry):
        rows = pl.ds(pl.multiple_of(r * rs, rs), rs)
        o_ref[rows, :] += jnp.dot(h_ref[rows, :], wb[...], preferred_element_type=F32)
        return carry

    lax.fori_loop(0, tm // rs, step, 0)


def _ffn_down(h, w2, lead, *, tn=1024, tm=TOKEN_SLAB, rs=ROW_SUB):
    M = h.shape[0]
    N = w2.shape[-1]
    last = D_FF // DOWN_WK - 1
    w_specs = [pl.BlockSpec((None,) * len(lead) + (DOWN_WK, tn),
                            lambda i, j, k, q=q: (*lead, jnp.minimum(k * DOWN_NW + q, last), j))
               for q in range(DOWN_NW)]
    return pl.pallas_call(
        functools.partial(_down_body, rs=rs, tm=tm),
        grid=(M // tm, N // tn, D_FF_PAD // DOWN_TK),
        in_specs=[pl.BlockSpec((tm, DOWN_TK), lambda i, j, k: (i, k))] + w_specs,
        out_specs=pl.BlockSpec((tm, tn), lambda i, j, k: (i, j)),
        out_shape=jax.ShapeDtypeStruct((M, N), F32),
        scratch_shapes=[pltpu.VMEM((DOWN_TK, tn), BF16)],
        compiler_params=_cparams(3),
        name="ffn_down",
    )(h, *([w2] * DOWN_NW))


def _layer_norm_rows(v, g, b):
    mu = jnp.mean(v, axis=-1, keepdims=True)
    d = v - mu
    var = jnp.mean(d * d, axis=-1, keepdims=True)
    return d * lax.rsqrt(var + LN_EPS) * g + b


def _ln_body(*refs, n_add):
    x_ref = refs[0]
    adds = refs[1:1 + n_add]
    g_ref, b_ref, o_ref, ob_ref = refs[1 + n_add:]
    v = DN_ALPHA * x_ref[...]
    for a in adds:
        v = v + a[...]
    y = _layer_norm_rows(v, g_ref[...], b_ref[...])
    o_ref[...] = y
    ob_ref[...] = y.astype(BF16)


LN_ROWS = 256


def _ln(x, adds, g, b):
    M, D = x.shape
    row = pl.BlockSpec((LN_ROWS, D), lambda i: (i, 0))
    vec = pl.BlockSpec((1, D), lambda i: (0, 0))
    return pl.pallas_call(
        functools.partial(_ln_body, n_add=len(adds)),
        grid=(M // LN_ROWS,),
        in_specs=[row] * (1 + len(adds)) + [vec, vec],
        out_specs=[row, row],
        out_shape=[jax.ShapeDtypeStruct((M, D), F32), jax.ShapeDtypeStruct((M, D), BF16)],
        compiler_params=_cparams(1),
        name="layer_norm",
    )(x, *adds, g.reshape(1, D), b.reshape(1, D))


def _rglru_core(xr_ref, yr_ref, cw_ref, cb_ref, wa_ref, ba_ref, wi_ref, bi_ref, lam_ref, hinj_ref,
                oa_ref, ubuf, a_scr, b_scr, h_scr, h0, *, Lb):
    u = xr_ref[...]
    ubuf[PAD_L:PAD_L + Lb, :] = u
    cw = cw_ref[...]
    xc = cb_ref[...] + cw[0:1, :] * ubuf[PAD_L - 3:PAD_L - 3 + Lb, :]
    xc = xc + cw[1:2, :] * ubuf[PAD_L - 2:PAD_L - 2 + Lb, :]
    xc = xc + cw[2:3, :] * ubuf[PAD_L - 1:PAD_L - 1 + Lb, :]
    xc = xc + cw[3:4, :] * u
    sp = jnp.maximum(-lam_ref[...], 0.0) + jnp.log1p(jnp.exp(-jnp.abs(lam_ref[...])))
    if hinj_ref is not None:
        real = (lax.broadcasted_iota(jnp.int32, (Lb, RNN_BLOCK), 0) & (PAD_L - 1)) >= PAD_L // 2
    for n in range(RNN_COLS // RNN_BLOCK):
        cs = slice(n * RNN_BLOCK, (n + 1) * RNN_BLOCK)
        xn = xc[:, cs]
        xb = xn.astype(BF16)
        ga = jnp.dot(xb, wa_ref[n].astype(BF16), preferred_element_type=F32) + ba_ref[:, cs]
        gi = jnp.dot(xb, wi_ref[n].astype(BF16), preferred_element_type=F32) + bi_ref[:, cs]
        log_a = -LRU_C * _sigmoid(ga) * sp[:, cs]
        a = jnp.exp(log_a)
        b = jnp.sqrt(-jnp.tanh(log_a) * (a * a + 1.0)) * _sigmoid(gi) * xn
        if hinj_ref is not None:
            a = jnp.where(real, a, 0.0)
            b = jnp.where(real, b, hinj_ref[:, cs])
        a_scr[:, cs] = a
        b_scr[:, cs] = b

    rowid = lax.broadcasted_iota(jnp.int32, (V7X_SUBLANES, RNN_COLS), 0)

    def group(g, hprev):
        rows = pl.ds(pl.multiple_of(g * V7X_SUBLANES, V7X_SUBLANES), V7X_SUBLANES)
        a = a_scr[rows, :]
        b = b_scr[rows, :]
        for d in (1, 2, 4):
            keep = rowid >= d
            a_sh = jnp.where(keep, pltpu.roll(a, d, 0), 1.0)
            b_sh = jnp.where(keep, pltpu.roll(b, d, 0), 0.0)
            b = a * b_sh + b
            a = a * a_sh
        h = a * hprev + b
        h_scr[rows, :] = h
        return h[V7X_SUBLANES - 1:V7X_SUBLANES, :]

    h_last = lax.fori_loop(0, Lb // V7X_SUBLANES, group, h0)
    oa_ref[...] = (jax.nn.gelu(yr_ref[...]) * h_scr[...]).astype(BF16)
    return h_last


def _rglru_prompt_body(xr_ref, yr_ref, cw_ref, cb_ref, wa_ref, ba_ref, wi_ref, bi_ref, lam_ref,
                       oa_ref, hl_ref, ubuf, a_scr, b_scr, h_scr, hc_scr, *, Lb):
    l = pl.program_id(2)

    @pl.when(l == 0)
    def _():
        ubuf[0:PAD_L, :] = jnp.zeros((PAD_L, RNN_COLS), F32)
        hc_scr[...] = jnp.zeros_like(hc_scr)

    @pl.when(l > 0)
    def _():
        ubuf[0:PAD_L, :] = ubuf[Lb:Lb + PAD_L, :]

    h_last = _rglru_core(xr_ref, yr_ref, cw_ref, cb_ref, wa_ref, ba_ref, wi_ref, bi_ref, lam_ref,
                         None, oa_ref, ubuf, a_scr, b_scr, h_scr, hc_scr[...], Lb=Lb)
    hc_scr[...] = h_last

    @pl.when(l == pl.num_programs(2) - 1)
    def _():
        hl_ref[0] = h_last


def _rglru_sample_body(xr_ref, yr_ref, cw_ref, cb_ref, wa_ref, ba_ref, wi_ref, bi_ref, lam_ref,
                       hinj_ref, oa_ref, hl_ref, ubuf, a_scr, b_scr, h_scr, *, Lb):
    ubuf[0:PAD_L, :] = jnp.zeros((PAD_L, RNN_COLS), F32)
    _rglru_core(xr_ref, yr_ref, cw_ref, cb_ref, wa_ref, ba_ref, wi_ref, bi_ref, lam_ref,
                hinj_ref, oa_ref, ubuf, a_scr, b_scr, h_scr, jnp.zeros((1, RNN_COLS), F32), Lb=Lb)
    hl_ref[...] = h_scr[...].reshape(Lb // PAD_L, PAD_L, RNN_COLS)[:, PAD_L - 1, :]


def _rglru_weight_specs(cb_of):
    gb = RNN_COLS // RNN_BLOCK
    vec = pl.BlockSpec((1, RNN_COLS), lambda *g: (0, cb_of(*g)))
    gate = pl.BlockSpec((gb, RNN_BLOCK, RNN_BLOCK), lambda *g: (cb_of(*g), 0, 0))
    return [pl.BlockSpec((CONV_W, RNN_COLS), lambda *g: (0, cb_of(*g))), vec, gate, vec, gate, vec, vec]


def _rglru_scratch(Lb):
    return [pltpu.VMEM((Lb + PAD_L, RNN_COLS), F32), pltpu.VMEM((Lb, RNN_COLS), F32),
            pltpu.VMEM((Lb, RNN_COLS), F32), pltpu.VMEM((Lb, RNN_COLS), F32)]


def _rglru_prompt(z, weights, *, B, L, Lb=SEQ_BLOCK):
    NL = L // Lb
    NC = D_RNN // RNN_COLS
    yo = D_RNN // RNN_COLS
    cb_of = lambda b, c, l: c
    out_a, h_last = pl.pallas_call(
        functools.partial(_rglru_prompt_body, Lb=Lb),
        grid=(B, NC, NL),
        in_specs=[pl.BlockSpec((Lb, RNN_COLS), lambda b, c, l: (b * NL + l, c)),
                  pl.BlockSpec((Lb, RNN_COLS), lambda b, c, l: (b * NL + l, c + yo))]
        + _rglru_weight_specs(cb_of),
        out_specs=[pl.BlockSpec((Lb, RNN_COLS), lambda b, c, l: (b * NL + l, c)),
                   pl.BlockSpec((1, 1, RNN_COLS), lambda b, c, l: (b, 0, c))],
        out_shape=[jax.ShapeDtypeStruct((B * L, D_RNN), BF16),
                   jax.ShapeDtypeStruct((B, 1, D_RNN), F32)],
        scratch_shapes=_rglru_scratch(Lb) + [pltpu.VMEM((1, RNN_COLS), F32)],
        compiler_params=_cparams(3),
        name="rglru_prompt",
    )(z, z, *weights)
    return out_a, h_last.reshape(B, D_RNN)


def _rglru_sample(zs, hinj, weights, *, Lb=SEQ_BLOCK):
    R = zs.shape[0]
    NC = D_RNN // RNN_COLS
    yo = D_RNN // RNN_COLS
    cb_of = lambda i, c: c
    return pl.pallas_call(
        functools.partial(_rglru_sample_body, Lb=Lb),
        grid=(R // Lb, NC),
        in_specs=[pl.BlockSpec((Lb, RNN_COLS), lambda i, c: (i, c)),
                  pl.BlockSpec((Lb, RNN_COLS), lambda i, c: (i, c + yo))]
        + _rglru_weight_specs(cb_of)
        + [pl.BlockSpec((Lb, RNN_COLS), lambda i, c: (i, c))],
        out_specs=[pl.BlockSpec((Lb, RNN_COLS), lambda i, c: (i, c)),
                   pl.BlockSpec((Lb // PAD_L, RNN_COLS), lambda i, c: (i, c))],
        out_shape=[jax.ShapeDtypeStruct((R, D_RNN), BF16),
                   jax.ShapeDtypeStruct((R // PAD_L, D_RNN), F32)],
        scratch_shapes=_rglru_scratch(Lb),
        compiler_params=_cparams(2),
        name="rglru_sample",
    )(zs, zs, *weights, hinj)


def _gla_core(q_ref, k_ref, v_ref, go_ref, al_ref, wal_ref, bal_ref, gn_ref, o_ref,
              state_in, state_out, qd_scr, ke_scr, dl_scr, oi_scr, dt_scr, *, Lb, C, carried):
    nC = Lb // C
    pre = jnp.dot(al_ref[...].astype(BF16), wal_ref[...].astype(BF16),
                  preferred_element_type=F32) + bal_ref[...]
    lg = _log_sigmoid(pre) * (1.0 / GLA_TAU)
    rin = lax.broadcasted_iota(jnp.int32, (Lb, D_QK), 0) & (C - 1)
    if not carried:
        lg = jnp.where(rin >= C // 2, lg, 0.0)
    bc = lg
    d = 1
    while d < C:
        bc = bc + jnp.where(rin >= d, pltpu.roll(bc, d, 0), 0.0)
        d *= 2
    bc3 = bc.reshape(nC, C, D_QK)
    bl = jnp.broadcast_to(bc3[:, C - 1:C, :], (nC, C, D_QK)).reshape(Lb, D_QK)
    mid = jnp.broadcast_to(bc3[:, C // 2 - 1:C // 2, :], (nC, C, D_QK)).reshape(Lb, D_QK)
    q = q_ref[...] * (GLA_DK ** -0.5)
    k = k_ref[...]
    qd_scr[...] = q * jnp.exp(bc)
    ke_scr[...] = k * jnp.exp(bl - bc)
    dl_scr[...] = jnp.exp(bl)
    e = bc - mid
    qm = (q * jnp.exp(e)).astype(BF16)
    km = (k * jnp.exp(-e)).astype(BF16)

    ti = lax.broadcasted_iota(jnp.int32, (Lb, Lb), 0)
    si = lax.broadcasted_iota(jnp.int32, (Lb, Lb), 1)
    causal = ((ti & ~(C - 1)) == (si & ~(C - 1))) & (si <= ti)
    for h in range(GLA_HEADS):
        hk = slice(h * GLA_DK, (h + 1) * GLA_DK)
        hv = slice(h * GLA_DV, (h + 1) * GLA_DV)
        sc = lax.dot_general(qm[:, hk], km[:, hk], (((1,), (1,)), ((), ())),
                             preferred_element_type=F32)
        sc = jnp.where(causal, sc, 0.0)
        oi_scr[:, hv] = jnp.dot(sc.astype(BF16), v_ref[:, hv].astype(BF16),
                                preferred_element_type=F32)

    dt_scr[...] = jnp.zeros_like(dt_scr)

    def chunk(c, carry):
        r0 = pl.multiple_of(c * C, C)
        rows = pl.ds(r0, C)
        dl_row = dl_scr[pl.ds(r0, 1), :]
        for h in range(GLA_HEADS):
            dt_scr[h:h + 1, :] = dl_row[:, h * GLA_DK:(h + 1) * GLA_DK]
        dec_t = dt_scr[...].T
        sidx = 0 if carried else c
        for h in range(GLA_HEADS):
            hk = slice(h * GLA_DK, (h + 1) * GLA_DK)
            hv = slice(h * GLA_DV, (h + 1) * GLA_DV)
            s_prev = state_in[sidx, h]
            oi_scr[rows, hv] += jnp.dot(qd_scr[rows, hk].astype(BF16), s_prev.astype(BF16),
                                        preferred_element_type=F32)
            kv = lax.dot_general(ke_scr[rows, hk].astype(BF16), v_ref[rows, hv].astype(BF16),
                                 (((0,), (0,)), ((), ())), preferred_element_type=F32)
            state_out[sidx, h] = s_prev * dec_t[:, h:h + 1] + kv
        return carry

    lax.fori_loop(0, nC, chunk, 0)

    for h in range(GLA_HEADS):
        hv = slice(h * GLA_DV, (h + 1) * GLA_DV)
        o = oi_scr[:, hv]
        o = o * lax.rsqrt(jnp.mean(o * o, axis=-1, keepdims=True) + LN_EPS)
        o_ref[:, hv] = ((o * gn_ref[:, hv]) * _silu(go_ref[:, hv])).astype(BF16)


def _gla_prompt_body(q_ref, k_ref, v_ref, go_ref, al_ref, wal_ref, bal_ref, gn_ref, o_ref, so_ref,
                     qd_scr, ke_scr, dl_scr, oi_scr, dt_scr, *, Lb, C):
    @pl.when(pl.program_id(1) == 0)
    def _():
        so_ref[...] = jnp.zeros_like(so_ref)

    _gla_core(q_ref, k_ref, v_ref, go_ref, al_ref, wal_ref, bal_ref, gn_ref, o_ref, so_ref, so_ref,
              qd_scr, ke_scr, dl_scr, oi_scr, dt_scr, Lb=Lb, C=C, carried=True)


def _gla_sample_body(q_ref, k_ref, v_ref, go_ref, al_ref, wal_ref, bal_ref, gn_ref, s0_ref, o_ref,
                     so_ref, qd_scr, ke_scr, dl_scr, oi_scr, dt_scr, *, Lb, C):
    _gla_core(q_ref, k_ref, v_ref, go_ref, al_ref, wal_ref, bal_ref, gn_ref, o_ref, s0_ref, so_ref,
              qd_scr, ke_scr, dl_scr, oi_scr, dt_scr, Lb=Lb, C=C, carried=False)


def _gla_scratch(Lb):
    return [pltpu.VMEM((Lb, D_QK), F32), pltpu.VMEM((Lb, D_QK), F32), pltpu.VMEM((Lb, D_QK), F32),
            pltpu.VMEM((Lb, D_V), F32), pltpu.VMEM((V7X_LANES, V7X_LANES), F32)]


_Q_COL = 2 * D_RNN
_K_COL = _Q_COL + D_QK
_V_COL = _K_COL + D_QK
_G_COL = _V_COL + D_V


def _gla_in_specs(Lb, row_of):
    full = lambda shape: pl.BlockSpec(shape, lambda *g: (0,) * len(shape))
    return [pl.BlockSpec((Lb, D_QK), lambda *g: (row_of(*g), _Q_COL // D_QK)),
            pl.BlockSpec((Lb, D_QK), lambda *g: (row_of(*g), _K_COL // D_QK)),
            pl.BlockSpec((Lb, D_V), lambda *g: (row_of(*g), _V_COL // D_V)),
            pl.BlockSpec((Lb, D_V), lambda *g: (row_of(*g), _G_COL // D_V)),
            pl.BlockSpec((Lb, V7X_LANES), lambda *g: (row_of(*g), 0)),
            full((V7X_LANES, D_QK)), full((1, D_QK)), full((1, D_V))]


def _gla_prompt(z, alow, weights, *, B, L, Lb=SEQ_BLOCK, C=GLA_CHUNK):
    NL = L // Lb
    row_of = lambda b, l: b * NL + l
    state = (1, GLA_HEADS, GLA_DK, GLA_DV)
    return pl.pallas_call(
        functools.partial(_gla_prompt_body, Lb=Lb, C=C),
        grid=(B, NL),
        in_specs=_gla_in_specs(Lb, row_of),
        out_specs=[pl.BlockSpec((Lb, D_V), lambda b, l: (b * NL + l, 0)),
                   pl.BlockSpec(state, lambda b, l: (b, 0, 0, 0))],
        out_shape=[jax.ShapeDtypeStruct((B * L, D_V), BF16),
                   jax.ShapeDtypeStruct((B, GLA_HEADS, GLA_DK, GLA_DV), F32)],
        scratch_shapes=_gla_scratch(Lb),
        compiler_params=_cparams(2),
        name="gla_prompt",
    )(z, z, z, z, alow, *weights)


GLA_SAMPLE_BATCH = 8


def _gla_sample(zs, alow_s, s0, layer, weights, *, nB=GLA_SAMPLE_BATCH):
    R = zs.shape[0]
    Lb = nB * PAD_L
    row_of = lambda i: i
    state = (nB, GLA_HEADS, GLA_DK, GLA_DV)
    st_spec = pl.BlockSpec(state, lambda i: (i, 0, 0, 0))
    s0_spec = pl.BlockSpec((None,) + state, lambda i: (layer, i, 0, 0, 0))
    return pl.pallas_call(
        functools.partial(_gla_sample_body, Lb=Lb, C=PAD_L),
        grid=(R // Lb,),
        in_specs=_gla_in_specs(Lb, row_of) + [s0_spec],
        out_specs=[pl.BlockSpec((Lb, D_V), lambda i: (i, 0)), st_spec],
        out_shape=[jax.ShapeDtypeStruct((R, D_V), BF16),
                   jax.ShapeDtypeStruct(s0.shape[1:], F32)],
        scratch_shapes=_gla_scratch(Lb),
        compiler_params=_cparams(1),
        name="gla_sample",
    )(zs, zs, zs, zs, alow_s, *weights, s0)


ROUTER_ROWS = 256


def _router_body(x_ref, w_ref, i_ref, p_ref):
    logits = jnp.dot(x_ref[...], w_ref[...], precision=lax.Precision.HIGHEST,
                     preferred_element_type=F32)
    lane = lax.broadcasted_iota(jnp.int32, logits.shape, 1)
    logits = jnp.where(lane < N_EXPERTS, logits, -jnp.inf)
    ex = jnp.exp(logits - jnp.max(logits, axis=-1, keepdims=True))
    probs = ex / jnp.sum(ex, axis=-1, keepdims=True)
    p1 = jnp.max(probs, axis=-1, keepdims=True)
    i1 = jnp.min(jnp.where(probs == p1, lane, V7X_LANES), axis=-1, keepdims=True)
    rest = jnp.where(lane == i1, -1.0, probs)
    p2 = jnp.max(rest, axis=-1, keepdims=True)
    i2 = jnp.min(jnp.where(rest == p2, lane, V7X_LANES), axis=-1, keepdims=True)
    tot = p1 + p2
    i_ref[...] = jnp.where(lane == 0, i1, jnp.where(lane == 1, i2, 0))
    p_ref[...] = jnp.where(lane == 0, p1 / tot, jnp.where(lane == 1, p2 / tot, 0.0))


def _router(x, w_router):
    M, D = x.shape
    wp = jnp.pad(w_router, ((0, 0), (0, V7X_LANES - N_EXPERTS)))
    out = pl.BlockSpec((ROUTER_ROWS, V7X_LANES), lambda i: (i, 0))
    return pl.pallas_call(
        _router_body,
        grid=(M // ROUTER_ROWS,),
        in_specs=[pl.BlockSpec((ROUTER_ROWS, D), lambda i: (i, 0)),
                  pl.BlockSpec((D, V7X_LANES), lambda i: (0, 0))],
        out_specs=[out, out],
        out_shape=[jax.ShapeDtypeStruct((M, V7X_LANES), jnp.int32),
                   jax.ShapeDtypeStruct((M, V7X_LANES), F32)],
        compiler_params=_cparams(1),
        name="moe_router",
    )(x, wp)


def _moe_plan(top_i, n_tiles):
    M = top_i.shape[0]
    e_flat = top_i.reshape(-1)
    onehot = (e_flat[:, None] == jnp.arange(N_EXPERTS, dtype=jnp.int32)[None, :]).astype(jnp.int32)
    csum = jnp.cumsum(onehot, axis=0)
    rank = jnp.sum(csum * onehot, axis=1) - 1
    counts = csum[-1]
    padded = ((counts + MOE_TILE - 1) // MOE_TILE) * MOE_TILE
    ends = jnp.cumsum(padded)
    starts = ends - padded
    pos = jnp.sum(starts[None, :] * onehot, axis=1) + rank
    src = jnp.zeros((n_tiles * MOE_TILE,), jnp.int32).at[pos].set(
        jnp.arange(2 * M, dtype=jnp.int32) // 2)
    n_used = ends[-1] // MOE_TILE
    tile_src = jnp.minimum(jnp.arange(n_tiles, dtype=jnp.int32), n_used - 1)
    tile_e = jnp.sum((tile_src[:, None] * MOE_TILE >= ends[None, :]).astype(jnp.int32), axis=1)
    tile_e = jnp.minimum(tile_e, N_EXPERTS - 1)
    return pos.reshape(M, 2), src, tile_src, tile_e, n_used.reshape(1)


GATHER_ROWS = 256


def _row_copy(src_hbm, row, buf, slot, sem):
    return pltpu.make_async_copy(src_hbm.at[pl.ds(row, 1), :], buf.at[pl.ds(slot, 1), :], sem)


def _gather_rows(idx_ref, src_hbm, buf, sem, n):
    def issue(r, carry):
        _row_copy(src_hbm, idx_ref[0, 0, r], buf, r, sem).start()
        return carry

    def drain(r, carry):
        _row_copy(src_hbm, 0, buf, r, sem).wait()
        return carry

    lax.fori_loop(0, n, issue, 0)
    lax.fori_loop(0, n, drain, 0)


def _dispatch_body(idx_ref, x_hbm, o_ref, buf, sem):
    _gather_rows(idx_ref, x_hbm, buf, sem, GATHER_ROWS)
    o_ref[...] = buf[...].astype(BF16)


def _dispatch(x, src):
    R = src.shape[0]
    D = x.shape[1]
    nt = R // GATHER_ROWS
    return pl.pallas_call(
        _dispatch_body,
        grid=(nt,),
        in_specs=[pl.BlockSpec((1, 1, GATHER_ROWS), lambda t: (t, 0, 0), memory_space=pltpu.SMEM),
                  pl.BlockSpec(memory_space=pl.ANY)],
        out_specs=pl.BlockSpec((GATHER_ROWS, D), lambda t: (t, 0)),
        out_shape=jax.ShapeDtypeStruct((R, D), BF16),
        scratch_shapes=[pltpu.VMEM((GATHER_ROWS, D), F32), pltpu.SemaphoreType.DMA(())],
        compiler_params=_cparams(1),
        name="moe_dispatch",
    )(src.reshape(nt, 1, GATHER_ROWS), x)


MOE_UP_TN = 256


def _moe_up_body(tsrc_ref, te_ref, nu_ref, x_ref, w1_ref, w3_ref, o_ref, wb1, wb3):
    n = pl.program_id(0)
    t = pl.program_id(1)
    prev = te_ref[jnp.maximum(t - 1, 0)]
    fresh = jnp.logical_or(t == 0, te_ref[t] != prev)

    @pl.when(fresh)
    def _():
        wb1[...] = w1_ref[...].astype(BF16)
        wb3[...] = w3_ref[...].astype(BF16)

    @pl.when(t < nu_ref[0])
    def _():
        xs = x_ref[...]
        a = jnp.dot(xs, wb1[...], preferred_element_type=F32)
        b = jnp.dot(xs, wb3[...], preferred_element_type=F32)
        val = _silu(a) * b
        col = n * MOE_UP_TN + lax.broadcasted_iota(jnp.int32, val.shape, 1)
        o_ref[...] = jnp.where(col < D_FF, val, 0.0).astype(BF16)

    @pl.when(t >= nu_ref[0])
    def _():
        o_ref[...] = jnp.zeros_like(o_ref)


def _moe_up(xs, w1, w3, layer, tile_src, tile_e, n_used):
    R, D = xs.shape
    T = R // MOE_TILE
    last = D_FF // MOE_UP_TN - 1
    w_spec = pl.BlockSpec((None, None, D, MOE_UP_TN),
                          lambda n, t, ts, te, nu: (layer, te[t], 0, jnp.minimum(n, last)))
    return pl.pallas_call(
        _moe_up_body,
        grid_spec=pltpu.PrefetchScalarGridSpec(
            num_scalar_prefetch=3,
            grid=(D_FF_PAD // MOE_UP_TN, T),
            in_specs=[pl.BlockSpec((MOE_TILE, D), lambda n, t, ts, te, nu: (ts[t], 0)),
                      w_spec, w_spec],
            out_specs=pl.BlockSpec((MOE_TILE, MOE_UP_TN), lambda n, t, ts, te, nu: (t, n)),
            scratch_shapes=[pltpu.VMEM((D, MOE_UP_TN), BF16), pltpu.VMEM((D, MOE_UP_TN), BF16)],
        ),
        out_shape=jax.ShapeDtypeStruct((R, D_FF_PAD), BF16),
        compiler_params=_cparams(2),
        name="moe_up",
    )(tile_src, tile_e, n_used, xs, w1, w3)


MOE_DOWN_TK = 512
MOE_DOWN_NW = MOE_DOWN_TK // DOWN_WK


def _moe_down_body(tsrc_ref, te_ref, nu_ref, h_ref, *refs):
    w_refs = refs[:MOE_DOWN_NW]
    o_ref = refs[MOE_DOWN_NW]
    t = pl.program_id(0)
    k = pl.program_id(1)

    @pl.when(k == 0)
    def _():
        o_ref[...] = jnp.zeros_like(o_ref)

    @pl.when(t < nu_ref[0])
    def _():
        for q, w_ref in enumerate(w_refs):
            hq = h_ref[:, q * DOWN_WK:(q + 1) * DOWN_WK]
            o_ref[...] += jnp.dot(hq, w_ref[...].astype(BF16), preferred_element_type=F32)


def _moe_down(hs, w2, layer, tile_src, tile_e, n_used):
    R = hs.shape[0]
    T = R // MOE_TILE
    N = w2.shape[-1]
    last = D_FF // DOWN_WK - 1
    w_specs = [pl.BlockSpec((None, None, DOWN_WK, N),
                            lambda t, k, ts, te, nu, q=q:
                            (layer, te[t], jnp.minimum(k * MOE_DOWN_NW + q, last), 0))
               for q in range(MOE_DOWN_NW)]
    return pl.pallas_call(
        _moe_down_body,
        grid_spec=pltpu.PrefetchScalarGridSpec(
            num_scalar_prefetch=3,
            grid=(T, D_FF_PAD // MOE_DOWN_TK),
            in_specs=[pl.BlockSpec((MOE_TILE, MOE_DOWN_TK), lambda t, k, ts, te, nu: (ts[t], k))]
            + w_specs,
            out_specs=pl.BlockSpec((MOE_TILE, N), lambda t, k, ts, te, nu: (t, 0)),
        ),
        out_shape=jax.ShapeDtypeStruct((R, N), F32),
        compiler_params=_cparams(2),
        name="moe_down",
    )(tile_src, tile_e, n_used, hs, *([w2] * MOE_DOWN_NW))


COMBINE_ROWS = 128


def _combine_ln_body(pos_ref, x_ref, ple_ref, p_ref, g_ref, b_ref, ys_hbm, o_ref, ob_ref, buf, sem):
    _gather_rows(pos_ref, ys_hbm, buf, sem, 2 * COMBINE_ROWS)
    p = p_ref[...]
    y = jnp.zeros((COMBINE_ROWS, D_MODEL), F32)
    y = y + p[:, 0:1] * buf[0:COMBINE_ROWS, :]
    y = y + p[:, 1:2] * buf[COMBINE_ROWS:2 * COMBINE_ROWS, :]
    v = DN_ALPHA * x_ref[...] + y + ple_ref[...]
    out = _layer_norm_rows(v, g_ref[...], b_ref[...])
    o_ref[...] = out
    ob_ref[...] = out.astype(BF16)


def _combine_ln(x, ple, ys, pos, top_p, g, b):
    M, D = x.shape
    nt = M // COMBINE_ROWS
    pos_t = pos.reshape(nt, COMBINE_ROWS, 2).transpose(0, 2, 1).reshape(nt, 1, 2 * COMBINE_ROWS)
    row = pl.BlockSpec((COMBINE_ROWS, D), lambda t: (t, 0))
    vec = pl.BlockSpec((1, D), lambda t: (0, 0))
    return pl.pallas_call(
        _combine_ln_body,
        grid=(nt,),
        in_specs=[pl.BlockSpec((1, 1, 2 * COMBINE_ROWS), lambda t: (t, 0, 0),
                               memory_space=pltpu.SMEM),
                  row, row, pl.BlockSpec((COMBINE_ROWS, V7X_LANES), lambda t: (t, 0)), vec, vec,
                  pl.BlockSpec(memory_space=pl.ANY)],
        out_specs=[row, row],
        out_shape=[jax.ShapeDtypeStruct((M, D), F32), jax.ShapeDtypeStruct((M, D), BF16)],
        scratch_shapes=[pltpu.VMEM((2 * COMBINE_ROWS, D), F32), pltpu.SemaphoreType.DMA(())],
        compiler_params=_cparams(1),
        name="moe_combine_ln",
    )(pos_t, x, ple, top_p, g.reshape(1, D), b.reshape(1, D), ys)


def _pad_groups(a, front):
    B, L, N = a.shape
    lead = jnp.zeros((B, PAD_L - L - front.shape[1], N), a.dtype)
    return jnp.concatenate([lead, front, a], axis=1).reshape(B * PAD_L, N)


def kernel(x_prompt, x_sample, state_conv, state_rglru_h, state_gla, p_prompt, p_sample, w_in, conv_w, conv_b, rglru_w_a, rglru_b_a, rglru_w_i, rglru_b_i, rglru_lambda, gla_w_alpha, gla_b_alpha, gla_norm, w_proj_a, w_proj_b, w_gate, b_gate, w_out, ln1_g, ln1_b, ln2_g, ln2_b, dense_w1, dense_w3, dense_w2, moe_router, moe_w1, moe_w3, moe_w2, w_ple, w_ple_gate):
    BP, LP, _ = x_prompt.shape
    BS, LS, _ = x_sample.shape
    MP = BP * LP
    MS = BS * LS
    M = MP + MS
    x = jnp.concatenate([x_prompt.reshape(MP, D_MODEL), x_sample.reshape(MS, D_MODEL)], axis=0)
    xb = x.astype(BF16)
    p_all = jnp.concatenate([p_prompt.reshape(DEPTH, MP, D_PLE), p_sample.reshape(DEPTH, MS, D_PLE)],
                            axis=1)
    n_moe_tiles = 2 * M // MOE_TILE + N_EXPERTS

    conv_p, h_p, s_p, conv_s, h_s, s_s = [], [], [], [], [], []
    for i in range(DEPTH):
        z = _linear(xb, [(w_in, (i,))], name="in_proj", n_out=Z_MAIN, tn=512,
                    epilogue=lambda a: a, out_dtype=F32)
        w_tail = jnp.pad(w_in[i, :, Z_MAIN:], ((0, 0), (0, V7X_LANES - GLA_RANK)))
        alow = _linear(xb, [(w_tail, ())], name="in_proj_lowrank", n_out=V7X_LANES, tn=V7X_LANES,
                       epilogue=lambda a: a, out_dtype=F32)
        gates = _linear(xb, [(w_gate, (i,))], name="merge_gates", n_out=2 * D_MODEL, tn=512,
                        extras=[(b_gate[i].reshape(1, -1), "row", 0)],
                        epilogue=lambda a, b: _sigmoid(a + b), out_dtype=F32)

        rnn_w = (conv_w[i], conv_b[i].reshape(1, -1), rglru_w_a[i], rglru_b_a[i].reshape(1, -1),
                 rglru_w_i[i], rglru_b_i[i].reshape(1, -1), rglru_lambda[i].reshape(1, -1))
        gla_w = (jnp.pad(gla_w_alpha[i], ((0, V7X_LANES - GLA_RANK), (0, 0))),
                 gla_b_alpha[i].reshape(1, -1), gla_norm[i].reshape(1, -1))

        oa_p, hl_p = _rglru_prompt(z, rnn_w, B=BP, L=LP)
        ob_p, sn_p = _gla_prompt(z, alow, gla_w, B=BP, L=LP)
        conv_p.append(z[:MP, :D_RNN].reshape(BP, LP, D_RNN)[:, LP - (CONV_W - 1):, :])
        h_p.append(hl_p)
        s_p.append(sn_p)

        zs = z[MP:].reshape(BS, LS, Z_MAIN)
        front = jnp.concatenate(
            [state_conv[i], jnp.zeros((BS, CONV_W - 1, Z_MAIN - D_RNN), F32)], axis=2)
        zs_pad = _pad_groups(zs, front)
        alow_s = _pad_groups(alow[MP:].reshape(BS, LS, V7X_LANES),
                             jnp.zeros((BS, 0, V7X_LANES), F32))
        hinj = _pad_groups(jnp.zeros((BS, LS, D_RNN), F32), state_rglru_h[i][:, None, :])
        oa_s, hl_s = _rglru_sample(zs_pad, hinj, rnn_w)
        ob_s, sn_s = _gla_sample(zs_pad, alow_s, state_gla, i, gla_w)
        conv_s.append(zs[:, LS - (CONV_W - 1):, :D_RNN])
        h_s.append(hl_s)
        s_s.append(sn_s)

        out_a = jnp.concatenate(
            [oa_p, oa_s.reshape(BS, PAD_L, D_RNN)[:, PAD_L - LS:].reshape(MS, D_RNN)], axis=0)
        out_b = jnp.concatenate(
            [ob_p, ob_s.reshape(BS, PAD_L, D_V)[:, PAD_L - LS:].reshape(MS, D_V)], axis=0)

        pa = _linear(out_a, [(w_proj_a, (i,))], name="proj_a", n_out=D_MODEL, tn=512,
                     extras=[(gates, "tile", 0)], epilogue=lambda a, g: g * a, out_dtype=F32)
        merged = _linear(out_b, [(w_proj_b, (i,))], name="proj_b_merge", n_out=D_MODEL, tn=512,
                         extras=[(gates, "tile", D_MODEL), (pa, "tile", 0)],
                         epilogue=lambda a, g, prev: prev + g * a, out_dtype=BF16)
        mix = _linear(merged, [(w_out, (i,))], name="out_proj", n_out=D_MODEL, tn=512,
                      epilogue=lambda a: a, out_dtype=F32)
        x, xb = _ln(x, [mix], ln1_g[i], ln1_b[i])

        pp = _linear(p_all[i], [(w_ple, (i,))], name="ple_proj", n_out=D_MODEL, tn=512,
                     epilogue=lambda a: a, out_dtype=F32)
        ple = _linear(xb, [(w_ple_gate, (i,))], name="ple_gate", n_out=D_MODEL, tn=256,
                      extras=[(pp, "tile", 0)], epilogue=lambda a, q: _sigmoid(a) * q,
                      out_dtype=F32)

        j = i // 2
        if i % 2 == 0:
            h = _linear(xb, [(dense_w1, (j,)), (dense_w3, (j,))], name="ffn_up", n_out=D_FF_PAD,
                        tn=256, n_valid=D_FF, epilogue=lambda a, b: _silu(a) * b, out_dtype=BF16)
            ffn = _ffn_down(h, dense_w2, (j,))
            x, xb = _ln(x, [ffn, ple], ln2_g[i], ln2_b[i])
        else:
            top_i, top_p = _router(x, moe_router[j])
            pos, src, tile_src, tile_e, n_used = _moe_plan(top_i[:, :2], n_moe_tiles)
            xs = _dispatch(x, src)
            hs = _moe_up(xs, moe_w1, moe_w3, j, tile_src, tile_e, n_used)
            ys = _moe_down(hs, moe_w2, j, tile_src, tile_e, n_used)
            x, xb = _combine_ln(x, ple, ys, pos, top_p, ln2_g[i], ln2_b[i])

    y_prompt = x[:MP].reshape(BP, LP, D_MODEL)
    y_sample = x[MP:].reshape(BS, LS, D_MODEL)
    return (y_prompt, y_sample, jnp.stack(conv_p), jnp.stack(h_p), jnp.stack(s_p),
            jnp.stack(conv_s), jnp.stack(h_s), jnp.stack(s_s))
```

```python
import functools

import jax
import jax.numpy as jnp
from jax import lax
from jax.experimental import pallas as pl
from jax.experimental.pallas import tpu as pltpu

F32 = jnp.float32
BF16 = jnp.bfloat16

D_MODEL = 4096
DEPTH = 4
D_RNN = D_MODEL // 2
RNN_BLOCKS = 16
RNN_BLOCK = D_RNN // RNN_BLOCKS
CONV_W = 4
LRU_C = 8.0
GLA_HEADS = 8
GLA_DK = D_MODEL // 4 // GLA_HEADS
GLA_DV = D_MODEL // 2 // GLA_HEADS
GLA_RANK = 16
GLA_TAU = 16.0
D_FF = ((8 * D_MODEL // 3 + 255) // 256) * 256
N_EXPERTS = 8
D_PLE = 256
DN_ALPHA = (2 * DEPTH) ** 0.25
LN_EPS = 1e-5
D_QK = GLA_HEADS * GLA_DK
D_V = GLA_HEADS * GLA_DV
Z_MAIN = 2 * D_RNN + 2 * D_QK + 2 * D_V

V7X_LANES = 128
V7X_SUBLANES = 8
V7X_VMEM_LIMIT = 56 * 1024 * 1024

TOKEN_SLAB = 2176
ROW_SUB = 544
D_FF_PAD = 11264
MOE_ALIGN = 256
MOE_ROWS = 2 * MOE_ALIGN
MOE_SLAB = 2560
SEQ_BLOCK = 256
RNN_COLS = 512
GLA_CHUNK = 32
PAD_L = 8


def _cparams(n_axes):
    return pltpu.CompilerParams(dimension_semantics=("arbitrary",) * n_axes,
                                vmem_limit_bytes=V7X_VMEM_LIMIT)


def _sigmoid(x):
    return 1.0 / (1.0 + jnp.exp(-x))


def _silu(x):
    return x * _sigmoid(x)


def _log_sigmoid(x):
    return jnp.minimum(x, 0.0) - jnp.log1p(jnp.exp(-jnp.abs(x)))


def _linear_body(*refs, nw, kinds, epilogue, rs, tm, tn, n_valid):
    x_ref = refs[0]
    w_refs = refs[1:1 + nw]
    ex_refs = refs[1 + nw:1 + nw + len(kinds)]
    o_ref = refs[1 + nw + len(kinds)]
    wb_refs = refs[2 + nw + len(kinds):]
    for w_ref, wb in zip(w_refs, wb_refs):
        wb[...] = w_ref[...].astype(BF16)
    j = pl.program_id(1)

    def step(r, carry):
        rows = pl.ds(pl.multiple_of(r * rs, rs), rs)
        xs = x_ref[rows, :].astype(BF16)
        accs = [jnp.dot(xs, wb[...], preferred_element_type=F32) for wb in wb_refs]
        ex = [e[...] if kind == "row" else e[rows, :] for e, kind in zip(ex_refs, kinds)]
        val = epilogue(*accs, *ex)
        if n_valid is not None:
            col = j * tn + lax.broadcasted_iota(jnp.int32, val.shape, 1)
            val = jnp.where(col < n_valid, val, 0.0)
        o_ref[rows, :] = val.astype(o_ref.dtype)
        return carry

    lax.fori_loop(0, tm // rs, step, 0)


def _linear(x, ws, *, name, n_out, tn, epilogue, out_dtype, w_col0=0, extras=(), n_valid=None,
            tm=TOKEN_SLAB, rs=ROW_SUB):
    M, K = x.shape
    assert M % tm == 0 and tm % rs == 0 and n_out % tn == 0 and w_col0 % tn == 0
    wo = w_col0 // tn
    if n_valid is None:
        w_col = lambda j: j + wo
    else:
        assert n_valid % tn == 0
        last = n_valid // tn - 1
        w_col = lambda j: jnp.minimum(j + wo, last)
    in_specs = [pl.BlockSpec((tm, K), lambda i, j: (i, 0), pipeline_mode=pl.Buffered(1))]
    in_specs += [pl.BlockSpec((None,) * len(lead) + (K, tn),
                              lambda i, j, lead=lead: (*lead, 0, w_col(j))) for _, lead in ws]
    kinds = []
    args = [x, *[w for w, _ in ws]]
    for arr, kind, col0 in extras:
        assert col0 % tn == 0
        co = col0 // tn
        if kind == "row":
            in_specs.append(pl.BlockSpec((1, tn), lambda i, j, co=co: (0, j + co)))
        else:
            in_specs.append(pl.BlockSpec((tm, tn), lambda i, j, co=co: (i, j + co)))
        kinds.append(kind)
        args.append(arr)
    body = functools.partial(_linear_body, nw=len(ws), kinds=tuple(kinds), epilogue=epilogue,
                             rs=rs, tm=tm, tn=tn, n_valid=n_valid)
    return pl.pallas_call(
        body,
        grid=(M // tm, n_out // tn),
        in_specs=in_specs,
        out_specs=pl.BlockSpec((tm, tn), lambda i, j: (i, j)),
        out_shape=jax.ShapeDtypeStruct((M, n_out), out_dtype),
        scratch_shapes=[pltpu.VMEM((K, tn), BF16) for _ in ws],
        compiler_params=_cparams(2),
        name=name,
    )(*args)


DOWN_TK = 1024
DOWN_WK = 256
DOWN_NW = DOWN_TK // DOWN_WK


def _down_body(h_ref, *refs, rs, tm):
    w_refs = refs[:DOWN_NW]
    o_ref = refs[DOWN_NW]
    wb = refs[DOWN_NW + 1]
    k = pl.program_id(2)
    for q, w_ref in enumerate(w_refs):
        wb[q * DOWN_WK:(q + 1) * DOWN_WK, :] = w_ref[...].astype(BF16)

    @pl.when(k == 0)
    def _():
        o_ref[...] = jnp.zeros_like(o_ref)

    def step(r, carry):
        rows = pl.ds(pl.multiple_of(r * rs, rs), rs)
        o_ref[rows, :] += jnp.dot(h_ref[rows, :], wb[...], preferred_element_type=F32)
        return carry

    lax.fori_loop(0, tm // rs, step, 0)


def _ffn_down(h, w2, lead, *, tn=1024, tm=TOKEN_SLAB, rs=ROW_SUB):
    M = h.shape[0]
    N = w2.shape[-1]
    last = D_FF // DOWN_WK - 1
    w_specs = [pl.BlockSpec((None,) * len(lead) + (DOWN_WK, tn),
                            lambda i, j, k, q=q: (*lead, jnp.minimum(k * DOWN_NW + q, last), j))
               for q in range(DOWN_NW)]
    return pl.pallas_call(
        functools.partial(_down_body, rs=rs, tm=tm),
        grid=(M // tm, N // tn, D_FF_PAD // DOWN_TK),
        in_specs=[pl.BlockSpec((tm, DOWN_TK), lambda i, j, k: (i, k))] + w_specs,
        out_specs=pl.BlockSpec((tm, tn), lambda i, j, k: (i, j)),
        out_shape=jax.ShapeDtypeStruct((M, N), F32),
        scratch_shapes=[pltpu.VMEM((DOWN_TK, tn), BF16)],
        compiler_params=_cparams(3),
        name="ffn_down",
    )(h, *([w2] * DOWN_NW))


def _layer_norm_rows(v, g, b):
    mu = jnp.mean(v, axis=-1, keepdims=True)
    d = v - mu
    var = jnp.mean(d * d, axis=-1, keepdims=True)
    return d * lax.rsqrt(var + LN_EPS) * g + b


def _ln_body(*refs, n_add):
    x_ref = refs[0]
    adds = refs[1:1 + n_add]
    g_ref, b_ref, o_ref, ob_ref = refs[1 + n_add:]
    v = DN_ALPHA * x_ref[...]
    for a in adds:
        v = v + a[...]
    y = _layer_norm_rows(v, g_ref[...], b_ref[...])
    o_ref[...] = y
    ob_ref[...] = y.astype(BF16)


LN_ROWS = 256


def _ln(x, adds, g, b):
    M, D = x.shape
    row = pl.BlockSpec((LN_ROWS, D), lambda i: (i, 0))
    vec = pl.BlockSpec((1, D), lambda i: (0, 0))
    return pl.pallas_call(
        functools.partial(_ln_body, n_add=len(adds)),
        grid=(M // LN_ROWS,),
        in_specs=[row] * (1 + len(adds)) + [vec, vec],
        out_specs=[row, row],
        out_shape=[jax.ShapeDtypeStruct((M, D), F32), jax.ShapeDtypeStruct((M, D), BF16)],
        compiler_params=_cparams(1),
        name="layer_norm",
    )(x, *adds, g.reshape(1, D), b.reshape(1, D))


def _rglru_core(xr_ref, yr_ref, cw_ref, cb_ref, wa_ref, ba_ref, wi_ref, bi_ref, lam_ref, hinj_ref,
                oa_ref, ubuf, a_scr, b_scr, h_scr, h0, *, Lb):
    u = xr_ref[...]
    ubuf[PAD_L:PAD_L + Lb, :] = u
    cw = cw_ref[...]
    xc = cb_ref[...] + cw[0:1, :] * ubuf[PAD_L - 3:PAD_L - 3 + Lb, :]
    xc = xc + cw[1:2, :] * ubuf[PAD_L - 2:PAD_L - 2 + Lb, :]
    xc = xc + cw[2:3, :] * ubuf[PAD_L - 1:PAD_L - 1 + Lb, :]
    xc = xc + cw[3:4, :] * u
    sp = jnp.maximum(-lam_ref[...], 0.0) + jnp.log1p(jnp.exp(-jnp.abs(lam_ref[...])))
    if hinj_ref is not None:
        real = (lax.broadcasted_iota(jnp.int32, (Lb, RNN_BLOCK), 0) & (PAD_L - 1)) >= PAD_L // 2
    for n in range(RNN_COLS // RNN_BLOCK):
        cs = slice(n * RNN_BLOCK, (n + 1) * RNN_BLOCK)
        xn = xc[:, cs]
        xb = xn.astype(BF16)
        ga = jnp.dot(xb, wa_ref[n].astype(BF16), preferred_element_type=F32) + ba_ref[:, cs]
        gi = jnp.dot(xb, wi_ref[n].astype(BF16), preferred_element_type=F32) + bi_ref[:, cs]
        log_a = -LRU_C * _sigmoid(ga) * sp[:, cs]
        a = jnp.exp(log_a)
        b = jnp.sqrt(-jnp.tanh(log_a) * (a * a + 1.0)) * _sigmoid(gi) * xn
        if hinj_ref is not None:
            a = jnp.where(real, a, 0.0)
            b = jnp.where(real, b, hinj_ref[:, cs])
        a_scr[:, cs] = a
        b_scr[:, cs] = b

    rowid = lax.broadcasted_iota(jnp.int32, (V7X_SUBLANES, RNN_COLS), 0)

    def group(g, hprev):
        rows = pl.ds(pl.multiple_of(g * V7X_SUBLANES, V7X_SUBLANES), V7X_SUBLANES)
        a = a_scr[rows, :]
        b = b_scr[rows, :]
        for d in (1, 2, 4):
            keep = rowid >= d
            a_sh = jnp.where(keep, pltpu.roll(a, d, 0), 1.0)
            b_sh = jnp.where(keep, pltpu.roll(b, d, 0), 0.0)
            b = a * b_sh + b
            a = a * a_sh
        h = a * hprev + b
        h_scr[rows, :] = h
        return h[V7X_SUBLANES - 1:V7X_SUBLANES, :]

    h_last = lax.fori_loop(0, Lb // V7X_SUBLANES, group, h0)
    oa_ref[...] = (jax.nn.gelu(yr_ref[...]) * h_scr[...]).astype(BF16)
    return h_last


def _rglru_prompt_body(xr_ref, yr_ref, cw_ref, cb_ref, wa_ref, ba_ref, wi_ref, bi_ref, lam_ref,
                       oa_ref, hl_ref, ubuf, a_scr, b_scr, h_scr, hc_scr, *, Lb):
    l = pl.program_id(2)

    @pl.when(l == 0)
    def _():
        ubuf[0:PAD_L, :] = jnp.zeros((PAD_L, RNN_COLS), F32)
        hc_scr[...] = jnp.zeros_like(hc_scr)

    @pl.when(l > 0)
    def _():
        ubuf[0:PAD_L, :] = ubuf[Lb:Lb + PAD_L, :]

    h_last = _rglru_core(xr_ref, yr_ref, cw_ref, cb_ref, wa_ref, ba_ref, wi_ref, bi_ref, lam_ref,
                         None, oa_ref, ubuf, a_scr, b_scr, h_scr, hc_scr[...], Lb=Lb)
    hc_scr[...] = h_last

    @pl.when(l == pl.num_programs(2) - 1)
    def _():
        hl_ref[0] = h_last


def _rglru_sample_body(xr_ref, yr_ref, cw_ref, cb_ref, wa_ref, ba_ref, wi_ref, bi_ref, lam_ref,
                       hinj_ref, oa_ref, hl_ref, ubuf, a_scr, b_scr, h_scr, *, Lb):
    ubuf[0:PAD_L, :] = jnp.zeros((PAD_L, RNN_COLS), F32)
    _rglru_core(xr_ref, yr_ref, cw_ref, cb_ref, wa_ref, ba_ref, wi_ref, bi_ref, lam_ref,
                hinj_ref, oa_ref, ubuf, a_scr, b_scr, h_scr, jnp.zeros((1, RNN_COLS), F32), Lb=Lb)
    hl_ref[...] = h_scr[...].reshape(Lb // PAD_L, PAD_L, RNN_COLS)[:, PAD_L - 1, :]


def _rglru_weight_specs(cb_of):
    gb = RNN_COLS // RNN_BLOCK
    vec = pl.BlockSpec((1, RNN_COLS), lambda *g: (0, cb_of(*g)))
    gate = pl.BlockSpec((gb, RNN_BLOCK, RNN_BLOCK), lambda *g: (cb_of(*g), 0, 0))
    return [pl.BlockSpec((CONV_W, RNN_COLS), lambda *g: (0, cb_of(*g))), vec, gate, vec, gate, vec, vec]


def _rglru_scratch(Lb):
    return [pltpu.VMEM((Lb + PAD_L, RNN_COLS), F32), pltpu.VMEM((Lb, RNN_COLS), F32),
            pltpu.VMEM((Lb, RNN_COLS), F32), pltpu.VMEM((Lb, RNN_COLS), F32)]


def _rglru_prompt(z, weights, *, B, L, Lb=SEQ_BLOCK):
    NL = L // Lb
    NC = D_RNN // RNN_COLS
    yo = D_RNN // RNN_COLS
    cb_of = lambda b, c, l: c
    out_a, h_last = pl.pallas_call(
        functools.partial(_rglru_prompt_body, Lb=Lb),
        grid=(B, NC, NL),
        in_specs=[pl.BlockSpec((Lb, RNN_COLS), lambda b, c, l: (b * NL + l, c)),
                  pl.BlockSpec((Lb, RNN_COLS), lambda b, c, l: (b * NL + l, c + yo))]
        + _rglru_weight_specs(cb_of),
        out_specs=[pl.BlockSpec((Lb, RNN_COLS), lambda b, c, l: (b * NL + l, c)),
                   pl.BlockSpec((1, 1, RNN_COLS), lambda b, c, l: (b, 0, c))],
        out_shape=[jax.ShapeDtypeStruct((B * L, D_RNN), BF16),
                   jax.ShapeDtypeStruct((B, 1, D_RNN), F32)],
        scratch_shapes=_rglru_scratch(Lb) + [pltpu.VMEM((1, RNN_COLS), F32)],
        compiler_params=_cparams(3),
        name="rglru_prompt",
    )(z, z, *weights)
    return out_a, h_last.reshape(B, D_RNN)


def _rglru_sample(zs, hinj, weights, *, Lb=SEQ_BLOCK):
    R = zs.shape[0]
    NC = D_RNN // RNN_COLS
    yo = D_RNN // RNN_COLS
    cb_of = lambda i, c: c
    return pl.pallas_call(
        functools.partial(_rglru_sample_body, Lb=Lb),
        grid=(R // Lb, NC),
        in_specs=[pl.BlockSpec((Lb, RNN_COLS), lambda i, c: (i, c)),
                  pl.BlockSpec((Lb, RNN_COLS), lambda i, c: (i, c + yo))]
        + _rglru_weight_specs(cb_of)
        + [pl.BlockSpec((Lb, RNN_COLS), lambda i, c: (i, c))],
        out_specs=[pl.BlockSpec((Lb, RNN_COLS), lambda i, c: (i, c)),
                   pl.BlockSpec((Lb // PAD_L, RNN_COLS), lambda i, c: (i, c))],
        out_shape=[jax.ShapeDtypeStruct((R, D_RNN), BF16),
                   jax.ShapeDtypeStruct((R // PAD_L, D_RNN), F32)],
        scratch_shapes=_rglru_scratch(Lb),
        compiler_params=_cparams(2),
        name="rglru_sample",
    )(zs, zs, *weights, hinj)


def _gla_core(q_ref, k_ref, v_ref, go_ref, al_ref, wal_ref, bal_ref, gn_ref, o_ref,
              state_in, state_out, qd_scr, ke_scr, dl_scr, oi_scr, dt_scr, *, Lb, C, carried):
    nC = Lb // C
    pre = jnp.dot(al_ref[...].astype(BF16), wal_ref[...].astype(BF16),
                  preferred_element_type=F32) + bal_ref[...]
    lg = _log_sigmoid(pre) * (1.0 / GLA_TAU)
    rin = lax.broadcasted_iota(jnp.int32, (Lb, D_QK), 0) & (C - 1)
    if not carried:
        lg = jnp.where(rin >= C // 2, lg, 0.0)
    bc = lg
    d = 1
    while d < C:
        bc = bc + jnp.where(rin >= d, pltpu.roll(bc, d, 0), 0.0)
        d *= 2
    bc3 = bc.reshape(nC, C, D_QK)
    bl = jnp.broadcast_to(bc3[:, C - 1:C, :], (nC, C, D_QK)).reshape(Lb, D_QK)
    mid = jnp.broadcast_to(bc3[:, C // 2 - 1:C // 2, :], (nC, C, D_QK)).reshape(Lb, D_QK)
    q = q_ref[...] * (GLA_DK ** -0.5)
    k = k_ref[...]
    qd_scr[...] = q * jnp.exp(bc)
    ke_scr[...] = k * jnp.exp(bl - bc)
    dl_scr[...] = jnp.exp(bl)
    e = bc - mid
    qm = (q * jnp.exp(e)).astype(BF16)
    km = (k * jnp.exp(-e)).astype(BF16)

    ti = lax.broadcasted_iota(jnp.int32, (Lb, Lb), 0)
    si = lax.broadcasted_iota(jnp.int32, (Lb, Lb), 1)
    causal = ((ti & ~(C - 1)) == (si & ~(C - 1))) & (si <= ti)
    for h in range(GLA_HEADS):
        hk = slice(h * GLA_DK, (h + 1) * GLA_DK)
        hv = slice(h * GLA_DV, (h + 1) * GLA_DV)
        sc = lax.dot_general(qm[:, hk], km[:, hk], (((1,), (1,)), ((), ())),
                             preferred_element_type=F32)
        sc = jnp.where(causal, sc, 0.0)
        oi_scr[:, hv] = jnp.dot(sc.astype(BF16), v_ref[:, hv].astype(BF16),
                                preferred_element_type=F32)

    dt_scr[...] = jnp.zeros_like(dt_scr)

    def chunk(c, carry):
        r0 = pl.multiple_of(c * C, C)
        rows = pl.ds(r0, C)
        dl_row = dl_scr[pl.ds(r0, 1), :]
        for h in range(GLA_HEADS):
            dt_scr[h:h + 1, :] = dl_row[:, h * GLA_DK:(h + 1) * GLA_DK]
        dec_t = dt_scr[...].T
        sidx = 0 if carried else c
        for h in range(GLA_HEADS):
            hk = slice(h * GLA_DK, (h + 1) * GLA_DK)
            hv = slice(h * GLA_DV, (h + 1) * GLA_DV)
            s_prev = state_in[sidx, h]
            oi_scr[rows, hv] += jnp.dot(qd_scr[rows, hk].astype(BF16), s_prev.astype(BF16),
                                        preferred_element_type=F32)
            kv = lax.dot_general(ke_scr[rows, hk].astype(BF16), v_ref[rows, hv].astype(BF16),
                                 (((0,), (0,)), ((), ())), preferred_element_type=F32)
            state_out[sidx, h] = s_prev * dec_t[:, h:h + 1] + kv
        return carry

    lax.fori_loop(0, nC, chunk, 0)

    for h in range(GLA_HEADS):
        hv = slice(h * GLA_DV, (h + 1) * GLA_DV)
        o = oi_scr[:, hv]
        o = o * lax.rsqrt(jnp.mean(o * o, axis=-1, keepdims=True) + LN_EPS)
        o_ref[:, hv] = ((o * gn_ref[:, hv]) * _silu(go_ref[:, hv])).astype(BF16)


def _gla_prompt_body(q_ref, k_ref, v_ref, go_ref, al_ref, wal_ref, bal_ref, gn_ref, o_ref, so_ref,
                     qd_scr, ke_scr, dl_scr, oi_scr, dt_scr, *, Lb, C):
    @pl.when(pl.program_id(1) == 0)
    def _():
        so_ref[...] = jnp.zeros_like(so_ref)

    _gla_core(q_ref, k_ref, v_ref, go_ref, al_ref, wal_ref, bal_ref, gn_ref, o_ref, so_ref, so_ref,
              qd_scr, ke_scr, dl_scr, oi_scr, dt_scr, Lb=Lb, C=C, carried=True)


def _gla_sample_body(q_ref, k_ref, v_ref, go_ref, al_ref, wal_ref, bal_ref, gn_ref, s0_ref, o_ref,
                     so_ref, qd_scr, ke_scr, dl_scr, oi_scr, dt_scr, *, Lb, C):
    _gla_core(q_ref, k_ref, v_ref, go_ref, al_ref, wal_ref, bal_ref, gn_ref, o_ref, s0_ref, so_ref,
              qd_scr, ke_scr, dl_scr, oi_scr, dt_scr, Lb=Lb, C=C, carried=False)


def _gla_scratch(Lb):
    return [pltpu.VMEM((Lb, D_QK), F32), pltpu.VMEM((Lb, D_QK), F32), pltpu.VMEM((Lb, D_QK), F32),
            pltpu.VMEM((Lb, D_V), F32), pltpu.VMEM((V7X_LANES, V7X_LANES), F32)]


_Q_COL = 2 * D_RNN
_K_COL = _Q_COL + D_QK
_V_COL = _K_COL + D_QK
_G_COL = _V_COL + D_V


def _gla_in_specs(Lb, row_of):
    full = lambda shape: pl.BlockSpec(shape, lambda *g: (0,) * len(shape))
    return [pl.BlockSpec((Lb, D_QK), lambda *g: (row_of(*g), _Q_COL // D_QK)),
            pl.BlockSpec((Lb, D_QK), lambda *g: (row_of(*g), _K_COL // D_QK)),
            pl.BlockSpec((Lb, D_V), lambda *g: (row_of(*g), _V_COL // D_V)),
            pl.BlockSpec((Lb, D_V), lambda *g: (row_of(*g), _G_COL // D_V)),
            pl.BlockSpec((Lb, V7X_LANES), lambda *g: (row_of(*g), 0)),
            full((V7X_LANES, D_QK)), full((1, D_QK)), full((1, D_V))]


def _gla_prompt(z, alow, weights, *, B, L, Lb=SEQ_BLOCK, C=GLA_CHUNK):
    NL = L // Lb
    row_of = lambda b, l: b * NL + l
    state = (1, GLA_HEADS, GLA_DK, GLA_DV)
    return pl.pallas_call(
        functools.partial(_gla_prompt_body, Lb=Lb, C=C),
        grid=(B, NL),
        in_specs=_gla_in_specs(Lb, row_of),
        out_specs=[pl.BlockSpec((Lb, D_V), lambda b, l: (b * NL + l, 0)),
                   pl.BlockSpec(state, lambda b, l: (b, 0, 0, 0))],
        out_shape=[jax.ShapeDtypeStruct((B * L, D_V), BF16),
                   jax.ShapeDtypeStruct((B, GLA_HEADS, GLA_DK, GLA_DV), F32)],
        scratch_shapes=_gla_scratch(Lb),
        compiler_params=_cparams(2),
        name="gla_prompt",
    )(z, z, z, z, alow, *weights)


GLA_SAMPLE_BATCH = 8


def _gla_sample(zs, alow_s, s0, layer, weights, *, nB=GLA_SAMPLE_BATCH):
    R = zs.shape[0]
    Lb = nB * PAD_L
    row_of = lambda i: i
    state = (nB, GLA_HEADS, GLA_DK, GLA_DV)
    st_spec = pl.BlockSpec(state, lambda i: (i, 0, 0, 0))
    s0_spec = pl.BlockSpec((None,) + state, lambda i: (layer, i, 0, 0, 0))
    return pl.pallas_call(
        functools.partial(_gla_sample_body, Lb=Lb, C=PAD_L),
        grid=(R // Lb,),
        in_specs=_gla_in_specs(Lb, row_of) + [s0_spec],
        out_specs=[pl.BlockSpec((Lb, D_V), lambda i: (i, 0)), st_spec],
        out_shape=[jax.ShapeDtypeStruct((R, D_V), BF16),
                   jax.ShapeDtypeStruct(s0.shape[1:], F32)],
        scratch_shapes=_gla_scratch(Lb),
        compiler_params=_cparams(1),
        name="gla_sample",
    )(zs, zs, zs, zs, alow_s, *weights, s0)


ROUTER_ROWS = 256


def _router_body(x_ref, w_ref, i_ref, p_ref):
    logits = jnp.dot(x_ref[...], w_ref[...], precision=lax.Precision.HIGHEST,
                     preferred_element_type=F32)
    lane = lax.broadcasted_iota(jnp.int32, logits.shape, 1)
    logits = jnp.where(lane < N_EXPERTS, logits, -jnp.inf)
    ex = jnp.exp(logits - jnp.max(logits, axis=-1, keepdims=True))
    probs = ex / jnp.sum(ex, axis=-1, keepdims=True)
    p1 = jnp.max(probs, axis=-1, keepdims=True)
    i1 = jnp.min(jnp.where(probs == p1, lane, V7X_LANES), axis=-1, keepdims=True)
    rest = jnp.where(lane == i1, -1.0, probs)
    p2 = jnp.max(rest, axis=-1, keepdims=True)
    i2 = jnp.min(jnp.where(rest == p2, lane, V7X_LANES), axis=-1, keepdims=True)
    tot = p1 + p2
    i_ref[...] = jnp.where(lane == 0, i1, jnp.where(lane == 1, i2, 0))
    p_ref[...] = jnp.where(lane == 0, p1 / tot, jnp.where(lane == 1, p2 / tot, 0.0))


def _router(x, w_router):
    M, D = x.shape
    wp = jnp.pad(w_router, ((0, 0), (0, V7X_LANES - N_EXPERTS)))
    out = pl.BlockSpec((ROUTER_ROWS, V7X_LANES), lambda i: (i, 0))
    return pl.pallas_call(
        _router_body,
        grid=(M // ROUTER_ROWS,),
        in_specs=[pl.BlockSpec((ROUTER_ROWS, D), lambda i: (i, 0)),
                  pl.BlockSpec((D, V7X_LANES), lambda i: (0, 0))],
        out_specs=[out, out],
        out_shape=[jax.ShapeDtypeStruct((M, V7X_LANES), jnp.int32),
                   jax.ShapeDtypeStruct((M, V7X_LANES), F32)],
        compiler_params=_cparams(1),
        name="moe_router",
    )(x, wp)


def _moe_max_slabs(n_assign):
    return (n_assign + N_EXPERTS * (MOE_ALIGN - 1) + N_EXPERTS * (MOE_SLAB - MOE_ALIGN)) // MOE_SLAB


def _moe_plan(top_i):
    M = top_i.shape[0]
    n_slabs = _moe_max_slabs(2 * M)
    e_flat = top_i.reshape(-1)
    onehot = (e_flat[:, None] == jnp.arange(N_EXPERTS, dtype=jnp.int32)[None, :]).astype(jnp.int32)
    csum = jnp.cumsum(onehot, axis=0)
    rank = jnp.sum(csum * onehot, axis=1) - 1
    counts = csum[-1]
    padded = ((counts + MOE_ALIGN - 1) // MOE_ALIGN) * MOE_ALIGN
    per_expert = (padded + MOE_SLAB - 1) // MOE_SLAB
    slab_end = jnp.cumsum(per_expert)
    slab_start = slab_end - per_expert
    n_used = slab_end[-1]
    pos = jnp.sum(slab_start[None, :] * onehot, axis=1) * MOE_SLAB + rank
    src = jnp.zeros((n_slabs * MOE_SLAB,), jnp.int32).at[pos].set(
        jnp.arange(2 * M, dtype=jnp.int32) // 2)
    p_idx = jnp.arange(n_slabs, dtype=jnp.int32)
    p_live = jnp.minimum(p_idx, n_used - 1)
    slab_expert = jnp.sum((p_live[:, None] >= slab_end[None, :]).astype(jnp.int32), axis=1)
    slab_expert = jnp.minimum(slab_expert, N_EXPERTS - 1)
    rows = padded[slab_expert] - (p_idx - slab_start[slab_expert]) * MOE_SLAB
    slab_rows = jnp.where(p_idx < n_used, jnp.clip(rows, 0, MOE_SLAB), 0).astype(jnp.int32)
    return pos.reshape(M, 2), src, slab_expert, slab_rows, n_used.reshape(1).astype(jnp.int32)


GATHER_ROWS = 256


def _row_copy(src_hbm, row, buf, slot, sem):
    return pltpu.make_async_copy(src_hbm.at[pl.ds(row, 1), :], buf.at[pl.ds(slot, 1), :], sem)


def _gather_rows(idx_ref, src_hbm, buf, sem, n):
    def issue(r, carry):
        _row_copy(src_hbm, idx_ref[0, 0, r], buf, r, sem).start()
        return carry

    def drain(r, carry):
        _row_copy(src_hbm, 0, buf, r, sem).wait()
        return carry

    lax.fori_loop(0, n, issue, 0)
    lax.fori_loop(0, n, drain, 0)


def _dispatch_body(nu_ref, idx_ref, x_hbm, o_ref, buf, sem):
    live = pl.program_id(0) // (MOE_SLAB // GATHER_ROWS) < nu_ref[0]

    @pl.when(live)
    def _():
        _gather_rows(idx_ref, x_hbm, buf, sem, GATHER_ROWS)
        o_ref[...] = buf[...].astype(BF16)

    @pl.when(jnp.logical_not(live))
    def _():
        o_ref[...] = jnp.zeros_like(o_ref)


def _dispatch(x, src, n_used):
    R = src.shape[0]
    D = x.shape[1]
    nt = R // GATHER_ROWS
    return pl.pallas_call(
        _dispatch_body,
        grid_spec=pltpu.PrefetchScalarGridSpec(
            num_scalar_prefetch=1,
            grid=(nt,),
            in_specs=[pl.BlockSpec((1, 1, GATHER_ROWS), lambda t, nu: (t, 0, 0),
                                   memory_space=pltpu.SMEM),
                      pl.BlockSpec(memory_space=pl.ANY)],
            out_specs=pl.BlockSpec((GATHER_ROWS, D), lambda t, nu: (t, 0)),
            scratch_shapes=[pltpu.VMEM((GATHER_ROWS, D), F32), pltpu.SemaphoreType.DMA(())],
        ),
        out_shape=jax.ShapeDtypeStruct((R, D), BF16),
        compiler_params=_cparams(1),
        name="moe_dispatch",
    )(n_used, src.reshape(nt, 1, GATHER_ROWS), x)


MOE_UP_TN = 256


def _slab_row_loop(n_rows, fn):
    n_full = n_rows // MOE_ROWS

    def full(i, carry):
        fn(pl.multiple_of(i * MOE_ROWS, MOE_ROWS), MOE_ROWS)
        return carry

    lax.fori_loop(0, n_full, full, 0)

    @pl.when(n_rows % MOE_ROWS != 0)
    def _():
        fn(pl.multiple_of(n_full * MOE_ROWS, MOE_ROWS), MOE_ALIGN)


def _moe_up_body(se_ref, sr_ref, nu_ref, x_ref, w1_ref, w3_ref, o_ref, wb1, wb3):
    p = pl.program_id(0)
    n = pl.program_id(1)
    live = jnp.logical_and(p < nu_ref[0], n < D_FF // MOE_UP_TN)

    @pl.when(live)
    def _():
        wb1[...] = w1_ref[...].astype(BF16)
        wb3[...] = w3_ref[...].astype(BF16)
        n_rows = sr_ref[p]

        def ffn(r0, nr):
            xs = x_ref[pl.ds(r0, nr), :]
            a = jnp.dot(xs, wb1[...], preferred_element_type=F32)
            b = jnp.dot(xs, wb3[...], preferred_element_type=F32)
            o_ref[pl.ds(r0, nr), :] = (_silu(a) * b).astype(BF16)

        _slab_row_loop(n_rows, ffn)

        def clear(i, carry):
            rows = pl.ds(pl.multiple_of(i * MOE_ALIGN, MOE_ALIGN), MOE_ALIGN)
            o_ref[rows, :] = jnp.zeros((MOE_ALIGN, MOE_UP_TN), BF16)
            return carry

        lax.fori_loop(n_rows // MOE_ALIGN, MOE_SLAB // MOE_ALIGN, clear, 0)

    @pl.when(jnp.logical_not(live))
    def _():
        o_ref[...] = jnp.zeros_like(o_ref)


def _moe_up(xs, w1, w3, layer, slab_e, slab_rows, n_used):
    R, D = xs.shape
    n_slabs = R // MOE_SLAB
    last = D_FF // MOE_UP_TN - 1

    def w_map(p, n, se, sr, nu):
        return (layer, se[p], 0, jnp.where(p < nu[0], jnp.minimum(n, last), last))

    w_spec = pl.BlockSpec((None, None, D, MOE_UP_TN), w_map)
    return pl.pallas_call(
        _moe_up_body,
        grid_spec=pltpu.PrefetchScalarGridSpec(
            num_scalar_prefetch=3,
            grid=(n_slabs, D_FF_PAD // MOE_UP_TN),
            in_specs=[pl.BlockSpec((MOE_SLAB, D),
                                   lambda p, n, se, sr, nu: (jnp.minimum(p, nu[0] - 1), 0),
                                   pipeline_mode=pl.Buffered(1)),
                      w_spec, w_spec],
            out_specs=pl.BlockSpec((MOE_SLAB, MOE_UP_TN), lambda p, n, se, sr, nu: (p, n)),
            scratch_shapes=[pltpu.VMEM((D, MOE_UP_TN), BF16), pltpu.VMEM((D, MOE_UP_TN), BF16)],
        ),
        out_shape=jax.ShapeDtypeStruct((R, D_FF_PAD), BF16),
        compiler_params=_cparams(2),
        name="moe_up",
    )(slab_e, slab_rows, n_used, xs, w1, w3)


MOE_DOWN_TN = 1024


def _moe_down_body(se_ref, sr_ref, nu_ref, h_ref, *refs):
    w_refs = refs[:DOWN_NW]
    o_ref = refs[DOWN_NW]
    wb = refs[DOWN_NW + 1]
    p = pl.program_id(0)

    @pl.when(pl.program_id(2) == 0)
    def _():
        o_ref[...] = jnp.zeros_like(o_ref)

    @pl.when(p < nu_ref[0])
    def _():
        for q, w_ref in enumerate(w_refs):
            wb[q * DOWN_WK:(q + 1) * DOWN_WK, :] = w_ref[...].astype(BF16)

        def acc(r0, nr):
            rows = pl.ds(r0, nr)
            o_ref[rows, :] += jnp.dot(h_ref[rows, :], wb[...], preferred_element_type=F32)

        _slab_row_loop(sr_ref[p], acc)


def _moe_down(hs, w2, layer, slab_e, slab_rows, n_used):
    R = hs.shape[0]
    n_slabs = R // MOE_SLAB
    N = w2.shape[-1]
    n_j = N // MOE_DOWN_TN
    n_k = D_FF_PAD // DOWN_TK
    last = D_FF // DOWN_WK - 1

    def h_map(p, j, k, se, sr, nu):
        return (jnp.minimum(p, nu[0] - 1), jnp.where(p < nu[0], k, n_k - 1))

    def w_map(p, j, k, se, sr, nu, q):
        live = p < nu[0]
        kq = jnp.where(live, k, n_k - 1) * DOWN_NW + q
        return (layer, se[p], jnp.minimum(kq, last), jnp.where(live, j, n_j - 1))

    w_specs = [pl.BlockSpec((None, None, DOWN_WK, MOE_DOWN_TN), functools.partial(w_map, q=q))
               for q in range(DOWN_NW)]
    return pl.pallas_call(
        _moe_down_body,
        grid_spec=pltpu.PrefetchScalarGridSpec(
            num_scalar_prefetch=3,
            grid=(n_slabs, n_j, n_k),
            in_specs=[pl.BlockSpec((MOE_SLAB, DOWN_TK), h_map)] + w_specs,
            out_specs=pl.BlockSpec((MOE_SLAB, MOE_DOWN_TN), lambda p, j, k, se, sr, nu: (p, j)),
            scratch_shapes=[pltpu.VMEM((DOWN_TK, MOE_DOWN_TN), BF16)],
        ),
        out_shape=jax.ShapeDtypeStruct((R, N), F32),
        compiler_params=_cparams(3),
        name="moe_down",
    )(slab_e, slab_rows, n_used, hs, *([w2] * DOWN_NW))


COMBINE_ROWS = 128


def _combine_ln_body(pos_ref, x_ref, ple_ref, p_ref, g_ref, b_ref, ys_hbm, o_ref, ob_ref, buf, sem):
    _gather_rows(pos_ref, ys_hbm, buf, sem, 2 * COMBINE_ROWS)
    p = p_ref[...]
    y = jnp.zeros((COMBINE_ROWS, D_MODEL), F32)
    y = y + p[:, 0:1] * buf[0:COMBINE_ROWS, :]
    y = y + p[:, 1:2] * buf[COMBINE_ROWS:2 * COMBINE_ROWS, :]
    v = DN_ALPHA * x_ref[...] + y + ple_ref[...]
    out = _layer_norm_rows(v, g_ref[...], b_ref[...])
    o_ref[...] = out
    ob_ref[...] = out.astype(BF16)


def _combine_ln(x, ple, ys, pos, top_p, g, b):
    M, D = x.shape
    nt = M // COMBINE_ROWS
    pos_t = pos.reshape(nt, COMBINE_ROWS, 2).transpose(0, 2, 1).reshape(nt, 1, 2 * COMBINE_ROWS)
    row = pl.BlockSpec((COMBINE_ROWS, D), lambda t: (t, 0))
    vec = pl.BlockSpec((1, D), lambda t: (0, 0))
    return pl.pallas_call(
        _combine_ln_body,
        grid=(nt,),
        in_specs=[pl.BlockSpec((1, 1, 2 * COMBINE_ROWS), lambda t: (t, 0, 0),
                               memory_space=pltpu.SMEM),
                  row, row, pl.BlockSpec((COMBINE_ROWS, V7X_LANES), lambda t: (t, 0)), vec, vec,
                  pl.BlockSpec(memory_space=pl.ANY)],
        out_specs=[row, row],
        out_shape=[jax.ShapeDtypeStruct((M, D), F32), jax.ShapeDtypeStruct((M, D), BF16)],
        scratch_shapes=[pltpu.VMEM((2 * COMBINE_ROWS, D), F32), pltpu.SemaphoreType.DMA(())],
        compiler_params=_cparams(1),
        name="moe_combine_ln",
    )(pos_t, x, ple, top_p, g.reshape(1, D), b.reshape(1, D), ys)


def _pad_groups(a, front):
    B, L, N = a.shape
    lead = jnp.zeros((B, PAD_L - L - front.shape[1], N), a.dtype)
    return jnp.concatenate([lead, front, a], axis=1).reshape(B * PAD_L, N)


def kernel(x_prompt, x_sample, state_conv, state_rglru_h, state_gla, p_prompt, p_sample, w_in, conv_w, conv_b, rglru_w_a, rglru_b_a, rglru_w_i, rglru_b_i, rglru_lambda, gla_w_alpha, gla_b_alpha, gla_norm, w_proj_a, w_proj_b, w_gate, b_gate, w_out, ln1_g, ln1_b, ln2_g, ln2_b, dense_w1, dense_w3, dense_w2, moe_router, moe_w1, moe_w3, moe_w2, w_ple, w_ple_gate):
    BP, LP, _ = x_prompt.shape
    BS, LS, _ = x_sample.shape
    MP = BP * LP
    MS = BS * LS
    M = MP + MS
    x = jnp.concatenate([x_prompt.reshape(MP, D_MODEL), x_sample.reshape(MS, D_MODEL)], axis=0)
    xb = x.astype(BF16)
    p_all = jnp.concatenate([p_prompt.reshape(DEPTH, MP, D_PLE), p_sample.reshape(DEPTH, MS, D_PLE)],
                            axis=1)

    conv_p, h_p, s_p, conv_s, h_s, s_s = [], [], [], [], [], []
    for i in range(DEPTH):
        z = _linear(xb, [(w_in, (i,))], name="in_proj", n_out=Z_MAIN, tn=512,
                    epilogue=lambda a: a, out_dtype=F32)
        w_tail = jnp.pad(w_in[i, :, Z_MAIN:], ((0, 0), (0, V7X_LANES - GLA_RANK)))
        alow = _linear(xb, [(w_tail, ())], name="in_proj_lowrank", n_out=V7X_LANES, tn=V7X_LANES,
                       epilogue=lambda a: a, out_dtype=F32)
        gates = _linear(xb, [(w_gate, (i,))], name="merge_gates", n_out=2 * D_MODEL, tn=512,
                        extras=[(b_gate[i].reshape(1, -1), "row", 0)],
                        epilogue=lambda a, b: _sigmoid(a + b), out_dtype=F32)

        rnn_w = (conv_w[i], conv_b[i].reshape(1, -1), rglru_w_a[i], rglru_b_a[i].reshape(1, -1),
                 rglru_w_i[i], rglru_b_i[i].reshape(1, -1), rglru_lambda[i].reshape(1, -1))
        gla_w = (jnp.pad(gla_w_alpha[i], ((0, V7X_LANES - GLA_RANK), (0, 0))),
                 gla_b_alpha[i].reshape(1, -1), gla_norm[i].reshape(1, -1))

        oa_p, hl_p = _rglru_prompt(z, rnn_w, B=BP, L=LP)
        ob_p, sn_p = _gla_prompt(z, alow, gla_w, B=BP, L=LP)
        conv_p.append(z[:MP, :D_RNN].reshape(BP, LP, D_RNN)[:, LP - (CONV_W - 1):, :])
        h_p.append(hl_p)
        s_p.append(sn_p)

        zs = z[MP:].reshape(BS, LS, Z_MAIN)
        front = jnp.concatenate(
            [state_conv[i], jnp.zeros((BS, CONV_W - 1, Z_MAIN - D_RNN), F32)], axis=2)
        zs_pad = _pad_groups(zs, front)
        alow_s = _pad_groups(alow[MP:].reshape(BS, LS, V7X_LANES),
                             jnp.zeros((BS, 0, V7X_LANES), F32))
        hinj = _pad_groups(jnp.zeros((BS, LS, D_RNN), F32), state_rglru_h[i][:, None, :])
        oa_s, hl_s = _rglru_sample(zs_pad, hinj, rnn_w)
        ob_s, sn_s = _gla_sample(zs_pad, alow_s, state_gla, i, gla_w)
        conv_s.append(zs[:, LS - (CONV_W - 1):, :D_RNN])
        h_s.append(hl_s)
        s_s.append(sn_s)

        out_a = jnp.concatenate(
            [oa_p, oa_s.reshape(BS, PAD_L, D_RNN)[:, PAD_L - LS:].reshape(MS, D_RNN)], axis=0)
        out_b = jnp.concatenate(
            [ob_p, ob_s.reshape(BS, PAD_L, D_V)[:, PAD_L - LS:].reshape(MS, D_V)], axis=0)

        pa = _linear(out_a, [(w_proj_a, (i,))], name="proj_a", n_out=D_MODEL, tn=512,
                     extras=[(gates, "tile", 0)], epilogue=lambda a, g: g * a, out_dtype=F32)
        merged = _linear(out_b, [(w_proj_b, (i,))], name="proj_b_merge", n_out=D_MODEL, tn=512,
                         extras=[(gates, "tile", D_MODEL), (pa, "tile", 0)],
                         epilogue=lambda a, g, prev: prev + g * a, out_dtype=BF16)
        mix = _linear(merged, [(w_out, (i,))], name="out_proj", n_out=D_MODEL, tn=512,
                      epilogue=lambda a: a, out_dtype=F32)
        x, xb = _ln(x, [mix], ln1_g[i], ln1_b[i])

        pp = _linear(p_all[i], [(w_ple, (i,))], name="ple_proj", n_out=D_MODEL, tn=512,
                     epilogue=lambda a: a, out_dtype=F32)
        ple = _linear(xb, [(w_ple_gate, (i,))], name="ple_gate", n_out=D_MODEL, tn=256,
                      extras=[(pp, "tile", 0)], epilogue=lambda a, q: _sigmoid(a) * q,
                      out_dtype=F32)

        j = i // 2
        if i % 2 == 0:
            h = _linear(xb, [(dense_w1, (j,)), (dense_w3, (j,))], name="ffn_up", n_out=D_FF_PAD,
                        tn=256, n_valid=D_FF, epilogue=lambda a, b: _silu(a) * b, out_dtype=BF16)
            ffn = _ffn_down(h, dense_w2, (j,))
            x, xb = _ln(x, [ffn, ple], ln2_g[i], ln2_b[i])
        else:
            top_i, top_p = _router(x, moe_router[j])
            pos, src, slab_e, slab_rows, n_used = _moe_plan(top_i[:, :2])
            xs = _dispatch(x, src, n_used)
            hs = _moe_up(xs, moe_w1, moe_w3, j, slab_e, slab_rows, n_used)
            ys = _moe_down(hs, moe_w2, j, slab_e, slab_rows, n_used)
            x, xb = _combine_ln(x, ple, ys, pos, top_p, ln2_g[i], ln2_b[i])

    y_prompt = x[:MP].reshape(BP, LP, D_MODEL)
    y_sample = x[MP:].reshape(BS, LS, D_MODEL)
    return (y_prompt, y_sample, jnp.stack(conv_p), jnp.stack(h_p), jnp.stack(s_p),
            jnp.stack(conv_s), jnp.stack(h_s), jnp.stack(s_s))
```

```python
import functools

import jax
import jax.numpy as jnp
from jax import lax
from jax.experimental import pallas as pl
from jax.experimental.pallas import tpu as pltpu

F32 = jnp.float32
BF16 = jnp.bfloat16

D_MODEL = 4096
DEPTH = 4
D_RNN = D_MODEL // 2
RNN_BLOCKS = 16
RNN_BLOCK = D_RNN // RNN_BLOCKS
CONV_W = 4
LRU_C = 8.0
GLA_HEADS = 8
GLA_DK = D_MODEL // 4 // GLA_HEADS
GLA_DV = D_MODEL // 2 // GLA_HEADS
GLA_RANK = 16
GLA_TAU = 16.0
D_FF = ((8 * D_MODEL // 3 + 255) // 256) * 256
N_EXPERTS = 8
D_PLE = 256
DN_ALPHA = (2 * DEPTH) ** 0.25
LN_EPS = 1e-5
D_QK = GLA_HEADS * GLA_DK
D_V = GLA_HEADS * GLA_DV
Z_MAIN = 2 * D_RNN + 2 * D_QK + 2 * D_V

V7X_LANES = 128
V7X_SUBLANES = 8
V7X_VMEM_LIMIT = 56 * 1024 * 1024

TOKEN_SLAB = 2176
ROW_SUB = 544
D_FF_PAD = 11264
MOE_ALIGN = 256
MOE_ROWS = 2 * MOE_ALIGN
MOE_SLAB = 2560
SEQ_BLOCK = 256
RNN_COLS = 512
GLA_CHUNK = 32
PAD_L = 8


def _cparams(n_axes):
    return pltpu.CompilerParams(dimension_semantics=("arbitrary",) * n_axes,
                                vmem_limit_bytes=V7X_VMEM_LIMIT)


def _sigmoid(x):
    return 1.0 / (1.0 + jnp.exp(-x))


def _silu(x):
    return x * _sigmoid(x)


def _log_sigmoid(x):
    return jnp.minimum(x, 0.0) - jnp.log1p(jnp.exp(-jnp.abs(x)))


def _linear_body(*refs, nx, x_of, kinds, epilogue, rs, tm, tn, n_valid):
    nw = len(x_of)
    x_refs = refs[:nx]
    w_refs = refs[nx:nx + nw]
    ex_refs = refs[nx + nw:nx + nw + len(kinds)]
    o_ref = refs[nx + nw + len(kinds)]
    wb_refs = refs[nx + nw + len(kinds) + 1:]
    for w_ref, wb in zip(w_refs, wb_refs):
        wb[...] = w_ref[...].astype(BF16)
    j = pl.program_id(1)

    def step(r, carry):
        rows = pl.ds(pl.multiple_of(r * rs, rs), rs)
        xs = [x_ref[rows, :].astype(BF16) for x_ref in x_refs]
        accs = [jnp.dot(xs[xi], wb[...], preferred_element_type=F32)
                for xi, wb in zip(x_of, wb_refs)]
        ex = [e[...] if kind == "row" else e[rows, :] for e, kind in zip(ex_refs, kinds)]
        val = epilogue(*accs, *ex)
        if n_valid is not None:
            col = j * tn + lax.broadcasted_iota(jnp.int32, val.shape, 1)
            val = jnp.where(col < n_valid, val, 0.0)
        o_ref[rows, :] = val.astype(o_ref.dtype)
        return carry

    lax.fori_loop(0, tm // rs, step, 0)


def _linear(xs, ws, *, name, n_out, tn, epilogue, out_dtype, w_col0=0, extras=(), n_valid=None,
            tm=TOKEN_SLAB, rs=ROW_SUB):
    M = xs[0].shape[0]
    assert all(x.shape[0] == M for x in xs)
    assert M % tm == 0 and tm % rs == 0 and n_out % tn == 0 and w_col0 % tn == 0
    wo = w_col0 // tn
    if n_valid is None:
        w_col = lambda j: j + wo
    else:
        assert n_valid % tn == 0
        last = n_valid // tn - 1
        w_col = lambda j: jnp.minimum(j + wo, last)
    in_specs = [pl.BlockSpec((tm, x.shape[1]), lambda i, j: (i, 0), pipeline_mode=pl.Buffered(1))
                for x in xs]
    in_specs += [pl.BlockSpec((None,) * len(lead) + (xs[xi].shape[1], tn),
                              lambda i, j, lead=lead: (*lead, 0, w_col(j))) for xi, _, lead in ws]
    kinds = []
    args = [*xs, *[w for _, w, _ in ws]]
    for arr, kind, col0 in extras:
        assert col0 % tn == 0
        co = col0 // tn
        if kind == "row":
            in_specs.append(pl.BlockSpec((1, tn), lambda i, j, co=co: (0, j + co)))
        else:
            in_specs.append(pl.BlockSpec((tm, tn), lambda i, j, co=co: (i, j + co)))
        kinds.append(kind)
        args.append(arr)
    body = functools.partial(_linear_body, nx=len(xs), x_of=tuple(xi for xi, _, _ in ws),
                             kinds=tuple(kinds), epilogue=epilogue, rs=rs, tm=tm, tn=tn,
                             n_valid=n_valid)
    return pl.pallas_call(
        body,
        grid=(M // tm, n_out // tn),
        in_specs=in_specs,
        out_specs=pl.BlockSpec((tm, tn), lambda i, j: (i, j)),
        out_shape=jax.ShapeDtypeStruct((M, n_out), out_dtype),
        scratch_shapes=[pltpu.VMEM((xs[xi].shape[1], tn), BF16) for xi, _, _ in ws],
        compiler_params=_cparams(2),
        name=name,
    )(*args)


DOWN_TK = 1024
DOWN_WK = 256
DOWN_NW = DOWN_TK // DOWN_WK


def _down_body(h_ref, *refs, rs, tm):
    w_refs = refs[:DOWN_NW]
    o_ref = refs[DOWN_NW]
    wb = refs[DOWN_NW + 1]
    k = pl.program_id(2)
    for q, w_ref in enumerate(w_refs):
        wb[q * DOWN_WK:(q + 1) * DOWN_WK, :] = w_ref[...].astype(BF16)

    @pl.when(k == 0)
    def _():
        o_ref[...] = jnp.zeros_like(o_ref)

    def step(r, carry):
        rows = pl.ds(pl.multiple_of(r * rs, rs), rs)
        o_ref[rows, :] += jnp.dot(h_ref[rows, :], wb[...], preferred_element_type=F32)
        return carry

    lax.fori_loop(0, tm // rs, step, 0)


def _ffn_down(h, w2, lead, *, tn=1024, tm=TOKEN_SLAB, rs=ROW_SUB):
    M = h.shape[0]
    N = w2.shape[-1]
    last = D_FF // DOWN_WK - 1
    w_specs = [pl.BlockSpec((None,) * len(lead) + (DOWN_WK, tn),
                            lambda i, j, k, q=q: (*lead, jnp.minimum(k * DOWN_NW + q, last), j))
               for q in range(DOWN_NW)]
    return pl.pallas_call(
        functools.partial(_down_body, rs=rs, tm=tm),
        grid=(M // tm, N // tn, D_FF_PAD // DOWN_TK),
        in_specs=[pl.BlockSpec((tm, DOWN_TK), lambda i, j, k: (i, k))] + w_specs,
        out_specs=pl.BlockSpec((tm, tn), lambda i, j, k: (i, j)),
        out_shape=jax.ShapeDtypeStruct((M, N), F32),
        scratch_shapes=[pltpu.VMEM((DOWN_TK, tn), BF16)],
        compiler_params=_cparams(3),
        name="ffn_down",
    )(h, *([w2] * DOWN_NW))


def _layer_norm_rows(v, g, b):
    mu = jnp.mean(v, axis=-1, keepdims=True)
    d = v - mu
    var = jnp.mean(d * d, axis=-1, keepdims=True)
    return d * lax.rsqrt(var + LN_EPS) * g + b


def _ln_body(*refs, n_add):
    x_ref = refs[0]
    adds = refs[1:1 + n_add]
    g_ref, b_ref, o_ref, ob_ref = refs[1 + n_add:]
    v = DN_ALPHA * x_ref[...]
    for a in adds:
        v = v + a[...]
    y = _layer_norm_rows(v, g_ref[...], b_ref[...])
    o_ref[...] = y
    ob_ref[...] = y.astype(BF16)


LN_ROWS = 256


def _ln(x, adds, g, b):
    M, D = x.shape
    row = pl.BlockSpec((LN_ROWS, D), lambda i: (i, 0))
    vec = pl.BlockSpec((1, D), lambda i: (0, 0))
    return pl.pallas_call(
        functools.partial(_ln_body, n_add=len(adds)),
        grid=(M // LN_ROWS,),
        in_specs=[row] * (1 + len(adds)) + [vec, vec],
        out_specs=[row, row],
        out_shape=[jax.ShapeDtypeStruct((M, D), F32), jax.ShapeDtypeStruct((M, D), BF16)],
        compiler_params=_cparams(1),
        name="layer_norm",
    )(x, *adds, g.reshape(1, D), b.reshape(1, D))


def _rglru_core(xr_ref, yr_ref, cw_ref, cb_ref, wa_ref, ba_ref, wi_ref, bi_ref, lam_ref, hinj_ref,
                oa_ref, ubuf, a_scr, b_scr, h_scr, h0, *, Lb):
    u = xr_ref[...]
    ubuf[PAD_L:PAD_L + Lb, :] = u
    cw = cw_ref[...]
    xc = cb_ref[...] + cw[0:1, :] * ubuf[PAD_L - 3:PAD_L - 3 + Lb, :]
    xc = xc + cw[1:2, :] * ubuf[PAD_L - 2:PAD_L - 2 + Lb, :]
    xc = xc + cw[2:3, :] * ubuf[PAD_L - 1:PAD_L - 1 + Lb, :]
    xc = xc + cw[3:4, :] * u
    sp = jnp.maximum(-lam_ref[...], 0.0) + jnp.log1p(jnp.exp(-jnp.abs(lam_ref[...])))
    if hinj_ref is not None:
        real = (lax.broadcasted_iota(jnp.int32, (Lb, RNN_BLOCK), 0) & (PAD_L - 1)) >= PAD_L // 2
    for n in range(RNN_COLS // RNN_BLOCK):
        cs = slice(n * RNN_BLOCK, (n + 1) * RNN_BLOCK)
        xn = xc[:, cs]
        xb = xn.astype(BF16)
        ga = jnp.dot(xb, wa_ref[n].astype(BF16), preferred_element_type=F32) + ba_ref[:, cs]
        gi = jnp.dot(xb, wi_ref[n].astype(BF16), preferred_element_type=F32) + bi_ref[:, cs]
        log_a = -LRU_C * _sigmoid(ga) * sp[:, cs]
        a = jnp.exp(log_a)
        b = jnp.sqrt(-jnp.tanh(log_a) * (a * a + 1.0)) * _sigmoid(gi) * xn
        if hinj_ref is not None:
            a = jnp.where(real, a, 0.0)
            b = jnp.where(real, b, hinj_ref[:, cs])
        a_scr[:, cs] = a
        b_scr[:, cs] = b

    rowid = lax.broadcasted_iota(jnp.int32, (V7X_SUBLANES, RNN_COLS), 0)

    def group(g, hprev):
        rows = pl.ds(pl.multiple_of(g * V7X_SUBLANES, V7X_SUBLANES), V7X_SUBLANES)
        a = a_scr[rows, :]
        b = b_scr[rows, :]
        for d in (1, 2, 4):
            keep = rowid >= d
            a_sh = jnp.where(keep, pltpu.roll(a, d, 0), 1.0)
            b_sh = jnp.where(keep, pltpu.roll(b, d, 0), 0.0)
            b = a * b_sh + b
            a = a * a_sh
        h = a * hprev + b
        h_scr[rows, :] = h
        return h[V7X_SUBLANES - 1:V7X_SUBLANES, :]

    h_last = lax.fori_loop(0, Lb // V7X_SUBLANES, group, h0)
    oa_ref[...] = (jax.nn.gelu(yr_ref[...]) * h_scr[...]).astype(BF16)
    return h_last


def _rglru_prompt_body(xr_ref, yr_ref, cw_ref, cb_ref, wa_ref, ba_ref, wi_ref, bi_ref, lam_ref,
                       oa_ref, hl_ref, ubuf, a_scr, b_scr, h_scr, hc_scr, *, Lb):
    l = pl.program_id(2)

    @pl.when(l == 0)
    def _():
        ubuf[0:PAD_L, :] = jnp.zeros((PAD_L, RNN_COLS), F32)
        hc_scr[...] = jnp.zeros_like(hc_scr)

    @pl.when(l > 0)
    def _():
        ubuf[0:PAD_L, :] = ubuf[Lb:Lb + PAD_L, :]

    h_last = _rglru_core(xr_ref, yr_ref, cw_ref, cb_ref, wa_ref, ba_ref, wi_ref, bi_ref, lam_ref,
                         None, oa_ref, ubuf, a_scr, b_scr, h_scr, hc_scr[...], Lb=Lb)
    hc_scr[...] = h_last

    @pl.when(l == pl.num_programs(2) - 1)
    def _():
        hl_ref[0] = h_last


def _rglru_sample_body(xr_ref, yr_ref, cw_ref, cb_ref, wa_ref, ba_ref, wi_ref, bi_ref, lam_ref,
                       hinj_ref, oa_ref, hl_ref, ubuf, a_scr, b_scr, h_scr, *, Lb):
    ubuf[0:PAD_L, :] = jnp.zeros((PAD_L, RNN_COLS), F32)
    _rglru_core(xr_ref, yr_ref, cw_ref, cb_ref, wa_ref, ba_ref, wi_ref, bi_ref, lam_ref,
                hinj_ref, oa_ref, ubuf, a_scr, b_scr, h_scr, jnp.zeros((1, RNN_COLS), F32), Lb=Lb)
    hl_ref[...] = h_scr[...].reshape(Lb // PAD_L, PAD_L, RNN_COLS)[:, PAD_L - 1, :]


def _rglru_weight_specs(cb_of):
    gb = RNN_COLS // RNN_BLOCK
    vec = pl.BlockSpec((1, RNN_COLS), lambda *g: (0, cb_of(*g)))
    gate = pl.BlockSpec((gb, RNN_BLOCK, RNN_BLOCK), lambda *g: (cb_of(*g), 0, 0))
    return [pl.BlockSpec((CONV_W, RNN_COLS), lambda *g: (0, cb_of(*g))), vec, gate, vec, gate, vec, vec]


def _rglru_scratch(Lb):
    return [pltpu.VMEM((Lb + PAD_L, RNN_COLS), F32), pltpu.VMEM((Lb, RNN_COLS), F32),
            pltpu.VMEM((Lb, RNN_COLS), F32), pltpu.VMEM((Lb, RNN_COLS), F32)]


def _rglru_prompt(z, weights, *, B, L, Lb=SEQ_BLOCK):
    NL = L // Lb
    NC = D_RNN // RNN_COLS
    yo = D_RNN // RNN_COLS
    cb_of = lambda b, c, l: c
    out_a, h_last = pl.pallas_call(
        functools.partial(_rglru_prompt_body, Lb=Lb),
        grid=(B, NC, NL),
        in_specs=[pl.BlockSpec((Lb, RNN_COLS), lambda b, c, l: (b * NL + l, c)),
                  pl.BlockSpec((Lb, RNN_COLS), lambda b, c, l: (b * NL + l, c + yo))]
        + _rglru_weight_specs(cb_of),
        out_specs=[pl.BlockSpec((Lb, RNN_COLS), lambda b, c, l: (b * NL + l, c)),
                   pl.BlockSpec((1, 1, RNN_COLS), lambda b, c, l: (b, 0, c))],
        out_shape=[jax.ShapeDtypeStruct((B * L, D_RNN), BF16),
                   jax.ShapeDtypeStruct((B, 1, D_RNN), F32)],
        scratch_shapes=_rglru_scratch(Lb) + [pltpu.VMEM((1, RNN_COLS), F32)],
        compiler_params=_cparams(3),
        name="rglru_prompt",
    )(z, z, *weights)
    return out_a, h_last.reshape(B, D_RNN)


def _rglru_sample(zs, hinj, weights, *, Lb=SEQ_BLOCK):
    R = zs.shape[0]
    NC = D_RNN // RNN_COLS
    yo = D_RNN // RNN_COLS
    cb_of = lambda i, c: c
    return pl.pallas_call(
        functools.partial(_rglru_sample_body, Lb=Lb),
        grid=(R // Lb, NC),
        in_specs=[pl.BlockSpec((Lb, RNN_COLS), lambda i, c: (i, c)),
                  pl.BlockSpec((Lb, RNN_COLS), lambda i, c: (i, c + yo))]
        + _rglru_weight_specs(cb_of)
        + [pl.BlockSpec((Lb, RNN_COLS), lambda i, c: (i, c))],
        out_specs=[pl.BlockSpec((Lb, RNN_COLS), lambda i, c: (i, c)),
                   pl.BlockSpec((Lb // PAD_L, RNN_COLS), lambda i, c: (i, c))],
        out_shape=[jax.ShapeDtypeStruct((R, D_RNN), BF16),
                   jax.ShapeDtypeStruct((R // PAD_L, D_RNN), F32)],
        scratch_shapes=_rglru_scratch(Lb),
        compiler_params=_cparams(2),
        name="rglru_sample",
    )(zs, zs, *weights, hinj)


def _gla_core(q_ref, k_ref, v_ref, go_ref, al_ref, wal_ref, bal_ref, gn_ref, o_ref,
              state_in, state_out, qd_scr, ke_scr, dl_scr, oi_scr, dt_scr, *, Lb, C, carried):
    nC = Lb // C
    pre = jnp.dot(al_ref[...].astype(BF16), wal_ref[...].astype(BF16),
                  preferred_element_type=F32) + bal_ref[...]
    lg = _log_sigmoid(pre) * (1.0 / GLA_TAU)
    rin = lax.broadcasted_iota(jnp.int32, (Lb, D_QK), 0) & (C - 1)
    if not carried:
        lg = jnp.where(rin >= C // 2, lg, 0.0)
    bc = lg
    d = 1
    while d < C:
        bc = bc + jnp.where(rin >= d, pltpu.roll(bc, d, 0), 0.0)
        d *= 2
    bc3 = bc.reshape(nC, C, D_QK)
    bl = jnp.broadcast_to(bc3[:, C - 1:C, :], (nC, C, D_QK)).reshape(Lb, D_QK)
    mid = jnp.broadcast_to(bc3[:, C // 2 - 1:C // 2, :], (nC, C, D_QK)).reshape(Lb, D_QK)
    q = q_ref[...] * (GLA_DK ** -0.5)
    k = k_ref[...]
    qd_scr[...] = q * jnp.exp(bc)
    ke_scr[...] = k * jnp.exp(bl - bc)
    dl_scr[...] = jnp.exp(bl)
    e = bc - mid
    qm = (q * jnp.exp(e)).astype(BF16)
    km = (k * jnp.exp(-e)).astype(BF16)

    ti = lax.broadcasted_iota(jnp.int32, (Lb, Lb), 0)
    si = lax.broadcasted_iota(jnp.int32, (Lb, Lb), 1)
    causal = ((ti & ~(C - 1)) == (si & ~(C - 1))) & (si <= ti)
    for h in range(GLA_HEADS):
        hk = slice(h * GLA_DK, (h + 1) * GLA_DK)
        hv = slice(h * GLA_DV, (h + 1) * GLA_DV)
        sc = lax.dot_general(qm[:, hk], km[:, hk], (((1,), (1,)), ((), ())),
                             preferred_element_type=F32)
        sc = jnp.where(causal, sc, 0.0)
        oi_scr[:, hv] = jnp.dot(sc.astype(BF16), v_ref[:, hv].astype(BF16),
                                preferred_element_type=F32)

    dt_scr[...] = jnp.zeros_like(dt_scr)

    def chunk(c, carry):
        r0 = pl.multiple_of(c * C, C)
        rows = pl.ds(r0, C)
        dl_row = dl_scr[pl.ds(r0, 1), :]
        for h in range(GLA_HEADS):
            dt_scr[h:h + 1, :] = dl_row[:, h * GLA_DK:(h + 1) * GLA_DK]
        dec_t = dt_scr[...].T
        sidx = 0 if carried else c
        for h in range(GLA_HEADS):
            hk = slice(h * GLA_DK, (h + 1) * GLA_DK)
            hv = slice(h * GLA_DV, (h + 1) * GLA_DV)
            s_prev = state_in[sidx, h]
            oi_scr[rows, hv] += jnp.dot(qd_scr[rows, hk].astype(BF16), s_prev.astype(BF16),
                                        preferred_element_type=F32)
            kv = lax.dot_general(ke_scr[rows, hk].astype(BF16), v_ref[rows, hv].astype(BF16),
                                 (((0,), (0,)), ((), ())), preferred_element_type=F32)
            state_out[sidx, h] = s_prev * dec_t[:, h:h + 1] + kv
        return carry

    lax.fori_loop(0, nC, chunk, 0)

    for h in range(GLA_HEADS):
        hv = slice(h * GLA_DV, (h + 1) * GLA_DV)
        o = oi_scr[:, hv]
        o = o * lax.rsqrt(jnp.mean(o * o, axis=-1, keepdims=True) + LN_EPS)
        o_ref[:, hv] = ((o * gn_ref[:, hv]) * _silu(go_ref[:, hv])).astype(BF16)


def _gla_prompt_body(q_ref, k_ref, v_ref, go_ref, al_ref, wal_ref, bal_ref, gn_ref, o_ref, so_ref,
                     qd_scr, ke_scr, dl_scr, oi_scr, dt_scr, *, Lb, C):
    @pl.when(pl.program_id(1) == 0)
    def _():
        so_ref[...] = jnp.zeros_like(so_ref)

    _gla_core(q_ref, k_ref, v_ref, go_ref, al_ref, wal_ref, bal_ref, gn_ref, o_ref, so_ref, so_ref,
              qd_scr, ke_scr, dl_scr, oi_scr, dt_scr, Lb=Lb, C=C, carried=True)


def _gla_sample_body(q_ref, k_ref, v_ref, go_ref, al_ref, wal_ref, bal_ref, gn_ref, s0_ref, o_ref,
                     so_ref, qd_scr, ke_scr, dl_scr, oi_scr, dt_scr, *, Lb, C):
    _gla_core(q_ref, k_ref, v_ref, go_ref, al_ref, wal_ref, bal_ref, gn_ref, o_ref, s0_ref, so_ref,
              qd_scr, ke_scr, dl_scr, oi_scr, dt_scr, Lb=Lb, C=C, carried=False)


def _gla_scratch(Lb):
    return [pltpu.VMEM((Lb, D_QK), F32), pltpu.VMEM((Lb, D_QK), F32), pltpu.VMEM((Lb, D_QK), F32),
            pltpu.VMEM((Lb, D_V), F32), pltpu.VMEM((V7X_LANES, V7X_LANES), F32)]


_Q_COL = 2 * D_RNN
_K_COL = _Q_COL + D_QK
_V_COL = _K_COL + D_QK
_G_COL = _V_COL + D_V


def _gla_in_specs(Lb, row_of):
    full = lambda shape: pl.BlockSpec(shape, lambda *g: (0,) * len(shape))
    return [pl.BlockSpec((Lb, D_QK), lambda *g: (row_of(*g), _Q_COL // D_QK)),
            pl.BlockSpec((Lb, D_QK), lambda *g: (row_of(*g), _K_COL // D_QK)),
            pl.BlockSpec((Lb, D_V), lambda *g: (row_of(*g), _V_COL // D_V)),
            pl.BlockSpec((Lb, D_V), lambda *g: (row_of(*g), _G_COL // D_V)),
            pl.BlockSpec((Lb, V7X_LANES), lambda *g: (row_of(*g), 0)),
            full((V7X_LANES, D_QK)), full((1, D_QK)), full((1, D_V))]


def _gla_prompt(z, alow, weights, *, B, L, Lb=SEQ_BLOCK, C=GLA_CHUNK):
    NL = L // Lb
    row_of = lambda b, l: b * NL + l
    state = (1, GLA_HEADS, GLA_DK, GLA_DV)
    return pl.pallas_call(
        functools.partial(_gla_prompt_body, Lb=Lb, C=C),
        grid=(B, NL),
        in_specs=_gla_in_specs(Lb, row_of),
        out_specs=[pl.BlockSpec((Lb, D_V), lambda b, l: (b * NL + l, 0)),
                   pl.BlockSpec(state, lambda b, l: (b, 0, 0, 0))],
        out_shape=[jax.ShapeDtypeStruct((B * L, D_V), BF16),
                   jax.ShapeDtypeStruct((B, GLA_HEADS, GLA_DK, GLA_DV), F32)],
        scratch_shapes=_gla_scratch(Lb),
        compiler_params=_cparams(2),
        name="gla_prompt",
    )(z, z, z, z, alow, *weights)


GLA_SAMPLE_BATCH = 8


def _gla_sample(zs, alow_s, s0, layer, weights, *, nB=GLA_SAMPLE_BATCH):
    R = zs.shape[0]
    Lb = nB * PAD_L
    row_of = lambda i: i
    state = (nB, GLA_HEADS, GLA_DK, GLA_DV)
    st_spec = pl.BlockSpec(state, lambda i: (i, 0, 0, 0))
    s0_spec = pl.BlockSpec((None,) + state, lambda i: (layer, i, 0, 0, 0))
    return pl.pallas_call(
        functools.partial(_gla_sample_body, Lb=Lb, C=PAD_L),
        grid=(R // Lb,),
        in_specs=_gla_in_specs(Lb, row_of) + [s0_spec],
        out_specs=[pl.BlockSpec((Lb, D_V), lambda i: (i, 0)), st_spec],
        out_shape=[jax.ShapeDtypeStruct((R, D_V), BF16),
                   jax.ShapeDtypeStruct(s0.shape[1:], F32)],
        scratch_shapes=_gla_scratch(Lb),
        compiler_params=_cparams(1),
        name="gla_sample",
    )(zs, zs, zs, zs, alow_s, *weights, s0)


ROUTER_ROWS = 256


def _router_body(x_ref, w_ref, i_ref, p_ref):
    logits = jnp.dot(x_ref[...], w_ref[...], precision=lax.Precision.HIGHEST,
                     preferred_element_type=F32)
    lane = lax.broadcasted_iota(jnp.int32, logits.shape, 1)
    logits = jnp.where(lane < N_EXPERTS, logits, -jnp.inf)
    ex = jnp.exp(logits - jnp.max(logits, axis=-1, keepdims=True))
    probs = ex / jnp.sum(ex, axis=-1, keepdims=True)
    p1 = jnp.max(probs, axis=-1, keepdims=True)
    i1 = jnp.min(jnp.where(probs == p1, lane, V7X_LANES), axis=-1, keepdims=True)
    rest = jnp.where(lane == i1, -1.0, probs)
    p2 = jnp.max(rest, axis=-1, keepdims=True)
    i2 = jnp.min(jnp.where(rest == p2, lane, V7X_LANES), axis=-1, keepdims=True)
    tot = p1 + p2
    i_ref[...] = jnp.where(lane == 0, i1, jnp.where(lane == 1, i2, 0))
    p_ref[...] = jnp.where(lane == 0, p1 / tot, jnp.where(lane == 1, p2 / tot, 0.0))


def _router(x, w_router):
    M, D = x.shape
    wp = jnp.pad(w_router, ((0, 0), (0, V7X_LANES - N_EXPERTS)))
    out = pl.BlockSpec((ROUTER_ROWS, V7X_LANES), lambda i: (i, 0))
    return pl.pallas_call(
        _router_body,
        grid=(M // ROUTER_ROWS,),
        in_specs=[pl.BlockSpec((ROUTER_ROWS, D), lambda i: (i, 0)),
                  pl.BlockSpec((D, V7X_LANES), lambda i: (0, 0))],
        out_specs=[out, out],
        out_shape=[jax.ShapeDtypeStruct((M, V7X_LANES), jnp.int32),
                   jax.ShapeDtypeStruct((M, V7X_LANES), F32)],
        compiler_params=_cparams(1),
        name="moe_router",
    )(x, wp)


def _moe_max_slabs(n_assign):
    return (n_assign + N_EXPERTS * (MOE_ALIGN - 1) + N_EXPERTS * (MOE_SLAB - MOE_ALIGN)) // MOE_SLAB


def _moe_plan(top_i):
    M = top_i.shape[0]
    n_slabs = _moe_max_slabs(2 * M)
    e_flat = top_i.reshape(-1)
    onehot = (e_flat[:, None] == jnp.arange(N_EXPERTS, dtype=jnp.int32)[None, :]).astype(jnp.int32)
    csum = jnp.cumsum(onehot, axis=0)
    rank = jnp.sum(csum * onehot, axis=1) - 1
    counts = csum[-1]
    padded = ((counts + MOE_ALIGN - 1) // MOE_ALIGN) * MOE_ALIGN
    per_expert = (padded + MOE_SLAB - 1) // MOE_SLAB
    slab_end = jnp.cumsum(per_expert)
    slab_start = slab_end - per_expert
    n_used = slab_end[-1]
    pos = jnp.sum(slab_start[None, :] * onehot, axis=1) * MOE_SLAB + rank
    src = jnp.zeros((n_slabs * MOE_SLAB,), jnp.int32).at[pos].set(
        jnp.arange(2 * M, dtype=jnp.int32) // 2)
    p_idx = jnp.arange(n_slabs, dtype=jnp.int32)
    p_live = jnp.minimum(p_idx, n_used - 1)
    slab_expert = jnp.sum((p_live[:, None] >= slab_end[None, :]).astype(jnp.int32), axis=1)
    slab_expert = jnp.minimum(slab_expert, N_EXPERTS - 1)
    rows = padded[slab_expert] - (p_idx - slab_start[slab_expert]) * MOE_SLAB
    slab_rows = jnp.where(p_idx < n_used, jnp.clip(rows, 0, MOE_SLAB), 0).astype(jnp.int32)
    return pos.reshape(M, 2), src, slab_expert, slab_rows, n_used.reshape(1).astype(jnp.int32)


GATHER_ROWS = 256


def _row_copy(src_hbm, row, buf, slot, sem):
    return pltpu.make_async_copy(src_hbm.at[pl.ds(row, 1), :], buf.at[pl.ds(slot, 1), :], sem)


def _gather_rows(idx_ref, src_hbm, buf, sem, n):
    def issue(r, carry):
        _row_copy(src_hbm, idx_ref[0, 0, r], buf, r, sem).start()
        return carry

    def drain(r, carry):
        _row_copy(src_hbm, 0, buf, r, sem).wait()
        return carry

    lax.fori_loop(0, n, issue, 0)
    lax.fori_loop(0, n, drain, 0)


def _dispatch_body(nu_ref, idx_ref, x_hbm, o_ref, buf, sem):
    live = pl.program_id(0) // (MOE_SLAB // GATHER_ROWS) < nu_ref[0]

    @pl.when(live)
    def _():
        _gather_rows(idx_ref, x_hbm, buf, sem, GATHER_ROWS)
        o_ref[...] = buf[...].astype(BF16)

    @pl.when(jnp.logical_not(live))
    def _():
        o_ref[...] = jnp.zeros_like(o_ref)


def _dispatch(x, src, n_used):
    R = src.shape[0]
    D = x.shape[1]
    nt = R // GATHER_ROWS
    return pl.pallas_call(
        _dispatch_body,
        grid_spec=pltpu.PrefetchScalarGridSpec(
            num_scalar_prefetch=1,
            grid=(nt,),
            in_specs=[pl.BlockSpec((1, 1, GATHER_ROWS), lambda t, nu: (t, 0, 0),
                                   memory_space=pltpu.SMEM),
                      pl.BlockSpec(memory_space=pl.ANY)],
            out_specs=pl.BlockSpec((GATHER_ROWS, D), lambda t, nu: (t, 0)),
            scratch_shapes=[pltpu.VMEM((GATHER_ROWS, D), F32), pltpu.SemaphoreType.DMA(())],
        ),
        out_shape=jax.ShapeDtypeStruct((R, D), BF16),
        compiler_params=_cparams(1),
        name="moe_dispatch",
    )(n_used, src.reshape(nt, 1, GATHER_ROWS), x)


MOE_UP_TN = 256


def _slab_row_loop(n_rows, fn):
    n_full = n_rows // MOE_ROWS

    def full(i, carry):
        fn(pl.multiple_of(i * MOE_ROWS, MOE_ROWS), MOE_ROWS)
        return carry

    lax.fori_loop(0, n_full, full, 0)

    @pl.when(n_rows % MOE_ROWS != 0)
    def _():
        fn(pl.multiple_of(n_full * MOE_ROWS, MOE_ROWS), MOE_ALIGN)


def _moe_up_body(se_ref, sr_ref, nu_ref, x_ref, w1_ref, w3_ref, o_ref, wb1, wb3):
    p = pl.program_id(0)
    n = pl.program_id(1)
    live = jnp.logical_and(p < nu_ref[0], n < D_FF // MOE_UP_TN)

    @pl.when(live)
    def _():
        wb1[...] = w1_ref[...].astype(BF16)
        wb3[...] = w3_ref[...].astype(BF16)
        n_rows = sr_ref[p]

        def ffn(r0, nr):
            xs = x_ref[pl.ds(r0, nr), :]
            a = jnp.dot(xs, wb1[...], preferred_element_type=F32)
            b = jnp.dot(xs, wb3[...], preferred_element_type=F32)
            o_ref[pl.ds(r0, nr), :] = (_silu(a) * b).astype(BF16)

        _slab_row_loop(n_rows, ffn)

        def clear(i, carry):
            rows = pl.ds(pl.multiple_of(i * MOE_ALIGN, MOE_ALIGN), MOE_ALIGN)
            o_ref[rows, :] = jnp.zeros((MOE_ALIGN, MOE_UP_TN), BF16)
            return carry

        lax.fori_loop(n_rows // MOE_ALIGN, MOE_SLAB // MOE_ALIGN, clear, 0)

    @pl.when(jnp.logical_not(live))
    def _():
        o_ref[...] = jnp.zeros_like(o_ref)


def _moe_up(xs, w1, w3, layer, slab_e, slab_rows, n_used):
    R, D = xs.shape
    n_slabs = R // MOE_SLAB
    last = D_FF // MOE_UP_TN - 1

    def w_map(p, n, se, sr, nu):
        return (layer, se[p], 0, jnp.where(p < nu[0], jnp.minimum(n, last), last))

    w_spec = pl.BlockSpec((None, None, D, MOE_UP_TN), w_map)
    return pl.pallas_call(
        _moe_up_body,
        grid_spec=pltpu.PrefetchScalarGridSpec(
            num_scalar_prefetch=3,
            grid=(n_slabs, D_FF_PAD // MOE_UP_TN),
            in_specs=[pl.BlockSpec((MOE_SLAB, D),
                                   lambda p, n, se, sr, nu: (jnp.minimum(p, nu[0] - 1), 0),
                                   pipeline_mode=pl.Buffered(1)),
                      w_spec, w_spec],
            out_specs=pl.BlockSpec((MOE_SLAB, MOE_UP_TN), lambda p, n, se, sr, nu: (p, n)),
            scratch_shapes=[pltpu.VMEM((D, MOE_UP_TN), BF16), pltpu.VMEM((D, MOE_UP_TN), BF16)],
        ),
        out_shape=jax.ShapeDtypeStruct((R, D_FF_PAD), BF16),
        compiler_params=_cparams(2),
        name="moe_up",
    )(slab_e, slab_rows, n_used, xs, w1, w3)


MOE_DOWN_TN = 1024


def _moe_down_body(se_ref, sr_ref, nu_ref, h_ref, *refs):
    w_refs = refs[:DOWN_NW]
    o_ref = refs[DOWN_NW]
    wb = refs[DOWN_NW + 1]
    p = pl.program_id(0)

    @pl.when(pl.program_id(2) == 0)
    def _():
        o_ref[...] = jnp.zeros_like(o_ref)

    @pl.when(p < nu_ref[0])
    def _():
        for q, w_ref in enumerate(w_refs):
            wb[q * DOWN_WK:(q + 1) * DOWN_WK, :] = w_ref[...].astype(BF16)

        def acc(r0, nr):
            rows = pl.ds(r0, nr)
            o_ref[rows, :] += jnp.dot(h_ref[rows, :], wb[...], preferred_element_type=F32)

        _slab_row_loop(sr_ref[p], acc)


def _moe_down(hs, w2, layer, slab_e, slab_rows, n_used):
    R = hs.shape[0]
    n_slabs = R // MOE_SLAB
    N = w2.shape[-1]
    n_j = N // MOE_DOWN_TN
    n_k = D_FF_PAD // DOWN_TK
    last = D_FF // DOWN_WK - 1

    def h_map(p, j, k, se, sr, nu):
        return (jnp.minimum(p, nu[0] - 1), jnp.where(p < nu[0], k, n_k - 1))

    def w_map(p, j, k, se, sr, nu, q):
        live = p < nu[0]
        kq = jnp.where(live, k, n_k - 1) * DOWN_NW + q
        return (layer, se[p], jnp.minimum(kq, last), jnp.where(live, j, n_j - 1))

    w_specs = [pl.BlockSpec((None, None, DOWN_WK, MOE_DOWN_TN), functools.partial(w_map, q=q))
               for q in range(DOWN_NW)]
    return pl.pallas_call(
        _moe_down_body,
        grid_spec=pltpu.PrefetchScalarGridSpec(
            num_scalar_prefetch=3,
            grid=(n_slabs, n_j, n_k),
            in_specs=[pl.BlockSpec((MOE_SLAB, DOWN_TK), h_map)] + w_specs,
            out_specs=pl.BlockSpec((MOE_SLAB, MOE_DOWN_TN), lambda p, j, k, se, sr, nu: (p, j)),
            scratch_shapes=[pltpu.VMEM((DOWN_TK, MOE_DOWN_TN), BF16)],
        ),
        out_shape=jax.ShapeDtypeStruct((R, N), F32),
        compiler_params=_cparams(3),
        name="moe_down",
    )(slab_e, slab_rows, n_used, hs, *([w2] * DOWN_NW))


COMBINE_ROWS = 128


def _combine_ln_body(pos_ref, x_ref, ple_ref, p_ref, g_ref, b_ref, ys_hbm, o_ref, ob_ref, buf, sem):
    _gather_rows(pos_ref, ys_hbm, buf, sem, 2 * COMBINE_ROWS)
    p = p_ref[...]
    y = jnp.zeros((COMBINE_ROWS, D_MODEL), F32)
    y = y + p[:, 0:1] * buf[0:COMBINE_ROWS, :]
    y = y + p[:, 1:2] * buf[COMBINE_ROWS:2 * COMBINE_ROWS, :]
    v = DN_ALPHA * x_ref[...] + y + ple_ref[...]
    out = _layer_norm_rows(v, g_ref[...], b_ref[...])
    o_ref[...] = out
    ob_ref[...] = out.astype(BF16)


def _combine_ln(x, ple, ys, pos, top_p, g, b):
    M, D = x.shape
    nt = M // COMBINE_ROWS
    pos_t = pos.reshape(nt, COMBINE_ROWS, 2).transpose(0, 2, 1).reshape(nt, 1, 2 * COMBINE_ROWS)
    row = pl.BlockSpec((COMBINE_ROWS, D), lambda t: (t, 0))
    vec = pl.BlockSpec((1, D), lambda t: (0, 0))
    return pl.pallas_call(
        _combine_ln_body,
        grid=(nt,),
        in_specs=[pl.BlockSpec((1, 1, 2 * COMBINE_ROWS), lambda t: (t, 0, 0),
                               memory_space=pltpu.SMEM),
                  row, row, pl.BlockSpec((COMBINE_ROWS, V7X_LANES), lambda t: (t, 0)), vec, vec,
                  pl.BlockSpec(memory_space=pl.ANY)],
        out_specs=[row, row],
        out_shape=[jax.ShapeDtypeStruct((M, D), F32), jax.ShapeDtypeStruct((M, D), BF16)],
        scratch_shapes=[pltpu.VMEM((2 * COMBINE_ROWS, D), F32), pltpu.SemaphoreType.DMA(())],
        compiler_params=_cparams(1),
        name="moe_combine_ln",
    )(pos_t, x, ple, top_p, g.reshape(1, D), b.reshape(1, D), ys)


def _pad_groups(a, front):
    B, L, N = a.shape
    lead = jnp.zeros((B, PAD_L - L - front.shape[1], N), a.dtype)
    return jnp.concatenate([lead, front, a], axis=1).reshape(B * PAD_L, N)


def kernel(x_prompt, x_sample, state_conv, state_rglru_h, state_gla, p_prompt, p_sample, w_in, conv_w, conv_b, rglru_w_a, rglru_b_a, rglru_w_i, rglru_b_i, rglru_lambda, gla_w_alpha, gla_b_alpha, gla_norm, w_proj_a, w_proj_b, w_gate, b_gate, w_out, ln1_g, ln1_b, ln2_g, ln2_b, dense_w1, dense_w3, dense_w2, moe_router, moe_w1, moe_w3, moe_w2, w_ple, w_ple_gate):
    BP, LP, _ = x_prompt.shape
    BS, LS, _ = x_sample.shape
    MP = BP * LP
    MS = BS * LS
    M = MP + MS
    x = jnp.concatenate([x_prompt.reshape(MP, D_MODEL), x_sample.reshape(MS, D_MODEL)], axis=0)
    xb = x.astype(BF16)
    p_all = jnp.concatenate([p_prompt.reshape(DEPTH, MP, D_PLE), p_sample.reshape(DEPTH, MS, D_PLE)],
                            axis=1)

    conv_p, h_p, s_p, conv_s, h_s, s_s = [], [], [], [], [], []
    for i in range(DEPTH):
        z = _linear([xb], [(0, w_in, (i,))], name="in_proj", n_out=Z_MAIN, tn=512,
                    epilogue=lambda a: a, out_dtype=F32)
        w_tail = jnp.pad(w_in[i, :, Z_MAIN:], ((0, 0), (0, V7X_LANES - GLA_RANK)))
        alow = _linear([xb], [(0, w_tail, ())], name="in_proj_lowrank", n_out=V7X_LANES,
                       tn=V7X_LANES, epilogue=lambda a: a, out_dtype=F32)
        gates = _linear([xb], [(0, w_gate, (i,))], name="merge_gates", n_out=2 * D_MODEL, tn=512,
                        extras=[(b_gate[i].reshape(1, -1), "row", 0)],
                        epilogue=lambda a, b: _sigmoid(a + b), out_dtype=F32)

        rnn_w = (conv_w[i], conv_b[i].reshape(1, -1), rglru_w_a[i], rglru_b_a[i].reshape(1, -1),
                 rglru_w_i[i], rglru_b_i[i].reshape(1, -1), rglru_lambda[i].reshape(1, -1))
        gla_w = (jnp.pad(gla_w_alpha[i], ((0, V7X_LANES - GLA_RANK), (0, 0))),
                 gla_b_alpha[i].reshape(1, -1), gla_norm[i].reshape(1, -1))

        oa_p, hl_p = _rglru_prompt(z, rnn_w, B=BP, L=LP)
        ob_p, sn_p = _gla_prompt(z, alow, gla_w, B=BP, L=LP)
        conv_p.append(z[:MP, :D_RNN].reshape(BP, LP, D_RNN)[:, LP - (CONV_W - 1):, :])
        h_p.append(hl_p)
        s_p.append(sn_p)

        zs = z[MP:].reshape(BS, LS, Z_MAIN)
        front = jnp.concatenate(
            [state_conv[i], jnp.zeros((BS, CONV_W - 1, Z_MAIN - D_RNN), F32)], axis=2)
        zs_pad = _pad_groups(zs, front)
        alow_s = _pad_groups(alow[MP:].reshape(BS, LS, V7X_LANES),
                             jnp.zeros((BS, 0, V7X_LANES), F32))
        hinj = _pad_groups(jnp.zeros((BS, LS, D_RNN), F32), state_rglru_h[i][:, None, :])
        oa_s, hl_s = _rglru_sample(zs_pad, hinj, rnn_w)
        ob_s, sn_s = _gla_sample(zs_pad, alow_s, state_gla, i, gla_w)
        conv_s.append(zs[:, LS - (CONV_W - 1):, :D_RNN])
        h_s.append(hl_s)
        s_s.append(sn_s)

        out_a = jnp.concatenate(
            [oa_p, oa_s.reshape(BS, PAD_L, D_RNN)[:, PAD_L - LS:].reshape(MS, D_RNN)], axis=0)
        out_b = jnp.concatenate(
            [ob_p, ob_s.reshape(BS, PAD_L, D_V)[:, PAD_L - LS:].reshape(MS, D_V)], axis=0)

        merged = _linear([out_a, out_b], [(0, w_proj_a, (i,)), (1, w_proj_b, (i,))],
                         name="proj_merge", n_out=D_MODEL, tn=256,
                         extras=[(gates, "tile", 0), (gates, "tile", D_MODEL)],
                         epilogue=lambda a, b, ga, gb: ga * a + gb * b, out_dtype=BF16)
        mix = _linear([merged], [(0, w_out, (i,))], name="out_proj", n_out=D_MODEL, tn=512,
                      epilogue=lambda a: a, out_dtype=F32)
        x, xb = _ln(x, [mix], ln1_g[i], ln1_b[i])

        ple = _linear([xb, p_all[i]], [(0, w_ple_gate, (i,)), (1, w_ple, (i,))], name="ple",
                      n_out=D_MODEL, tn=512, epilogue=lambda a, q: _sigmoid(a) * q, out_dtype=F32)

        j = i // 2
        if i % 2 == 0:
            h = _linear([xb], [(0, dense_w1, (j,)), (0, dense_w3, (j,))], name="ffn_up",
                        n_out=D_FF_PAD,
                        tn=256, n_valid=D_FF, epilogue=lambda a, b: _silu(a) * b, out_dtype=BF16)
            ffn = _ffn_down(h, dense_w2, (j,))
            x, xb = _ln(x, [ffn, ple], ln2_g[i], ln2_b[i])
        else:
            top_i, top_p = _router(x, moe_router[j])
            pos, src, slab_e, slab_rows, n_used = _moe_plan(top_i[:, :2])
            xs = _dispatch(x, src, n_used)
            hs = _moe_up(xs, moe_w1, moe_w3, j, slab_e, slab_rows, n_used)
            ys = _moe_down(hs, moe_w2, j, slab_e, slab_rows, n_used)
            x, xb = _combine_ln(x, ple, ys, pos, top_p, ln2_g[i], ln2_b[i])

    y_prompt = x[:MP].reshape(BP, LP, D_MODEL)
    y_sample = x[MP:].reshape(BS, LS, D_MODEL)
    return (y_prompt, y_sample, jnp.stack(conv_p), jnp.stack(h_p), jnp.stack(s_p),
            jnp.stack(conv_s), jnp.stack(h_s), jnp.stack(s_s))
```
